```python
import jax
import jax.numpy as jnp
from jax import lax
import numpy as np

D_MODEL = 1024
BATCH = 4
SEQ = 4096
DEPTH = 4

GRID_W = 64
CTX_LEN = 256
D_MIX = 2 * D_MODEL
D_SSD = D_MIX // 2
SSD_HEADDIM = 64
SSD_HEADS = D_SSD // SSD_HEADDIM
SSD_GROUPS = 2
SSD_HEADS_PER_GROUP = SSD_HEADS // SSD_GROUPS
SSD_STATE = 128
SSD_CHUNK = 128
CONV_K = 3
D_CONV_CH = D_SSD + 2 * SSD_GROUPS * SSD_STATE
D_S5 = D_MIX // 4
S5_GROUP = 16
S5_GROUPS = D_S5 // S5_GROUP
S5_STATE = 64
D_FNET = D_MIX // 4
FNET_HEADS = 4
FNET_HEAD_DIM = D_FNET // FNET_HEADS
PROJ_SIZES = (D_SSD, D_CONV_CH, 2 * SSD_HEADS, D_S5, D_S5, D_FNET, D_FNET)
SPLIT_POINTS = tuple(int(v) for v in np.cumsum(PROJ_SIZES)[:-1])
D_IN_PROJ = sum(PROJ_SIZES)
EPS = 1e-6

kernel_name = 'hybrid_ssd_s5_fnet_prefix_block'


def rmsnorm(v, g):
    vf = v.astype(jnp.float32)
    vf = vf * lax.rsqrt(jnp.mean(vf * vf, axis=-1, keepdims=True) + EPS)
    return vf.astype(v.dtype) * g


def maybe_flip(t, rev):
    return jnp.flip(t, axis=1) if rev else t


def conv_grid(v, w, bias):
    b, l, ch = v.shape
    rows = l // GRID_W
    img = v.reshape(b, rows, GRID_W, ch)
    out = lax.conv_general_dilated(img, w[:, :, None, :], (1, 1), 'SAME',
                                   dimension_numbers=('NHWC', 'HWIO', 'NHWC'), feature_group_count=ch)
    return out.reshape(b, l, ch) + bias


def conv_seq(v, w_row, bias):
    ch = v.shape[-1]
    out = lax.conv_general_dilated(v, w_row[:, None, :], (1,), 'SAME',
                                   dimension_numbers=('NWC', 'WIO', 'NWC'), feature_group_count=ch)
    return out + bias


def ssd_scan(x, a, bm, cm, h0):
    b, l = x.shape[:2]
    q = SSD_CHUNK
    nc = l // q
    x = x.reshape(b, nc, q, SSD_GROUPS, SSD_HEADS_PER_GROUP, SSD_HEADDIM)
    a = a.reshape(b, nc, q, SSD_GROUPS, SSD_HEADS_PER_GROUP)
    bm = bm.reshape(b, nc, q, SSD_GROUPS, SSD_STATE)
    cm = cm.reshape(b, nc, q, SSD_GROUPS, SSD_STATE)
    a_cum = jnp.cumsum(a, axis=2)
    seg = a_cum[:, :, :, None] - a_cum[:, :, None, :]
    mask = jnp.tril(jnp.ones((q, q), bool))[:, :, None, None]
    decay = jnp.where(mask, jnp.exp(jnp.where(mask, seg, 0.0)), 0.0)
    cb = jnp.einsum('bcign,bcjgn->bcijg', cm, bm)
    y_diag = jnp.einsum('bcijgh,bcjghp->bcighp', cb[..., None] * decay, x)
    decay_to_end = jnp.exp(a_cum[:, :, -1:] - a_cum)
    states = jnp.einsum('bcjgn,bcjgh,bcjghp->bcghpn', bm, decay_to_end, x)
    chunk_decay = jnp.exp(a_cum[:, :, -1])

    def step(h, inp):
        st, dec = inp
        return h * dec[..., None, None] + st, h

    h_final, h_in = lax.scan(step, h0, (jnp.moveaxis(states, 1, 0), jnp.moveaxis(chunk_decay, 1, 0)))
    h_in = jnp.moveaxis(h_in, 0, 1)
    y_off = jnp.einsum('bcign,bcghpn->bcighp', cm, h_in) * jnp.exp(a_cum)[..., None]
    y = (y_diag + y_off).reshape(b, l, SSD_HEADS, SSD_HEADDIM)
    return y, h_final


def ssd_branch(zl, xbcl, dtl, zc, xbcc, dtc, conv_w, conv_b, dt_bias, a_log, d_skip, g_norm, need_ctx):
    f32 = jnp.float32
    b = zl.shape[0]
    xbcl = jax.nn.silu(conv_grid(xbcl, conv_w, conv_b))
    xbcc = jax.nn.silu(conv_seq(xbcc, conv_w[CONV_K // 2], conv_b))
    a_neg = -jnp.exp(a_log.astype(f32))

    def prep(xbc, dt_raw):
        l = xbc.shape[1]
        xs, bm, cm = jnp.split(xbc.astype(f32), [D_SSD, D_SSD + SSD_GROUPS * SSD_STATE], axis=-1)
        dt = jax.nn.softplus(dt_raw.astype(f32).reshape(b, l, 2, SSD_HEADS) + dt_bias.astype(f32))
        return (xs.reshape(b, l, SSD_HEADS, SSD_HEADDIM), bm.reshape(b, l, SSD_GROUPS, SSD_STATE),
                cm.reshape(b, l, SSD_GROUPS, SSD_STATE), dt)

    xl_, bl_, cl_, dtl_ = prep(xbcl, dtl)
    xc_, bc_, cc_, dtc_ = prep(xbcc, dtc)
    dsk = d_skip.astype(f32)[:, None]
    yl = dsk * xl_
    yc = dsk * xc_ if need_ctx else None
    h0 = jnp.zeros((b, SSD_GROUPS, SSD_HEADS_PER_GROUP, SSD_HEADDIM, SSD_STATE), f32)
    for d in range(2):
        rev = d == 1
        y_c, h_c = ssd_scan(maybe_flip(xc_ * dtc_[:, :, d, :, None], rev),
                            maybe_flip(dtc_[:, :, d] * a_neg[d], rev),
                            maybe_flip(bc_, rev), maybe_flip(cc_, rev), h0)
        y_l, _ = ssd_scan(maybe_flip(xl_ * dtl_[:, :, d, :, None], rev),
                          maybe_flip(dtl_[:, :, d] * a_neg[d], rev),
                          maybe_flip(bl_, rev), maybe_flip(cl_, rev), h_c)
        yl = yl + maybe_flip(y_l, rev)
        if need_ctx:
            yc = yc + maybe_flip(y_c, rev)

    def gated_norm(y, z):
        l = y.shape[1]
        v = (y.reshape(b, l, D_SSD) * jax.nn.silu(z.astype(f32))).reshape(b, l, SSD_GROUPS, D_SSD // SSD_GROUPS)
        v = v * lax.rsqrt(jnp.mean(v * v, axis=-1, keepdims=True) + EPS)
        return (v.reshape(b, l, D_SSD) * g_norm.astype(f32)).astype(z.dtype)

    out_l = gated_norm(yl, zl)
    out_c = gated_norm(yc, zc) if need_ctx else None
    return out_l, out_c


def s5_discretise(lam_re, lam_im, log_step, b_re, b_im):
    f32 = jnp.float32
    lr = lam_re.astype(f32)
    li = lam_im.astype(f32)
    step = jnp.exp(log_step.astype(f32))[:, None]
    mag = jnp.exp(lr * step)
    ar = mag * jnp.cos(li * step)
    ai = mag * jnp.sin(li * step)
    den = lr * lr + li * li
    fr = ((ar - 1.0) * lr + ai * li) / den
    fi = (ai * lr - (ar - 1.0) * li) / den
    br = b_re.astype(f32)
    bi = b_im.astype(f32)
    bbr = fr[..., None] * br - fi[..., None] * bi
    bbi = fr[..., None] * bi + fi[..., None] * br
    return ar, ai, bbr, bbi


def s5_scan(u, ar, ai, bbr, bbi, h0r, h0i):
    l = u.shape[1]
    xr = jnp.einsum('blgk,gpk->blgp', u, bbr)
    xi = jnp.einsum('blgk,gpk->blgp', u, bbi)
    xr = xr.at[:, 0].add(ar * h0r - ai * h0i)
    xi = xi.at[:, 0].add(ar * h0i + ai * h0r)
    a_r = jnp.broadcast_to(ar, (1, l) + ar.shape)
    a_i = jnp.broadcast_to(ai, (1, l) + ai.shape)

    def combine(e1, e2):
        a1r, a1i, b1r, b1i = e1
        a2r, a2i, b2r, b2i = e2
        return (a2r * a1r - a2i * a1i, a2r * a1i + a2i * a1r,
                a2r * b1r - a2i * b1i + b2r, a2r * b1i + a2i * b1r + b2i)

    _, _, hr, hi = lax.associative_scan(combine, (a_r, a_i, xr, xi), axis=1)
    return hr, hi


def s5_readout(hr, hi, c_re, c_im):
    b, l = hr.shape[:2]
    y = (jnp.einsum('blgp,gkp->blgk', hr, c_re.astype(jnp.float32))
         - jnp.einsum('blgp,gkp->blgk', hi, c_im.astype(jnp.float32)))
    return y.reshape(b, l, D_S5)


def s5_branch(ul, gl, uc, gc, lam_re, lam_im, log_step, b_re, b_im, c_re, c_im, d_skip,
              w_glu, b_glu, need_ctx):
    f32 = jnp.float32
    b = ul.shape[0]

    def grouped(u):
        return u.astype(f32).reshape(b, u.shape[1], S5_GROUPS, S5_GROUP)

    ulg = grouped(ul)
    ucg = grouped(uc)
    d32 = d_skip.astype(f32)
    yl = d32 * ul.astype(f32)
    yc = d32 * uc.astype(f32) if need_ctx else None
    h0 = jnp.zeros((b, S5_GROUPS, S5_STATE), f32)
    for d in range(2):
        rev = d == 1
        ar, ai, bbr, bbi = s5_discretise(lam_re[d], lam_im[d], log_step[d], b_re[d], b_im[d])
        hcr, hci = s5_scan(maybe_flip(ucg, rev), ar, ai, bbr, bbi, h0, h0)
        hlr, hli = s5_scan(maybe_flip(ulg, rev), ar, ai, bbr, bbi, hcr[:, -1], hci[:, -1])
        yl = yl + maybe_flip(s5_readout(hlr, hli, c_re[d], c_im[d]), rev)
        if need_ctx:
            yc = yc + maybe_flip(s5_readout(hcr, hci, c_re[d], c_im[d]), rev)

    def glu(y, g, dtype):
        v = jax.nn.gelu(y)
        out = v * jax.nn.sigmoid(v @ w_glu.astype(f32) + b_glu.astype(f32))
        return (out * jax.nn.silu(g.astype(f32))).astype(dtype)

    out_l = glu(yl, gl, ul.dtype)
    out_c = glu(yc, gc, uc.dtype) if need_ctx else None
    return out_l, out_c


def fnet_branch(u, g, w, bias):
    b, l, _ = u.shape
    spec = jnp.fft.fft2(u.astype(jnp.float32).reshape(b, l, FNET_HEADS, FNET_HEAD_DIM),
                        axes=(1, 3), norm='ortho').real
    mixed = spec.reshape(b, l, D_FNET).astype(u.dtype) @ w + bias
    return mixed * jax.nn.silu(g)


def hybrid_mixer(hl, hc, w_in, conv_w, conv_b, dt_bias, a_log, d_ssd, g_ssd_norm,
                 lam_re, lam_im, log_step, b_re, b_im, c_re, c_im, s5_d, w_glu, b_glu,
                 fnet_w, fnet_b, w_out, need_ctx):
    zl, xbcl, dtl, usl, gsl, ufl, gfl = jnp.split(hl @ w_in, SPLIT_POINTS, axis=-1)
    zc, xbcc, dtc, usc, gsc, ufc, gfc = jnp.split(hc @ w_in, SPLIT_POINTS, axis=-1)
    ssd_l, ssd_c = ssd_branch(zl, xbcl, dtl, zc, xbcc, dtc, conv_w, conv_b, dt_bias, a_log,
                              d_ssd, g_ssd_norm, need_ctx)
    s5_l, s5_c = s5_branch(usl, gsl, usc, gsc, lam_re, lam_im, log_step, b_re, b_im, c_re, c_im,
                           s5_d, w_glu, b_glu, need_ctx)
    out_l = jnp.concatenate([ssd_l, s5_l, fnet_branch(ufl, gfl, fnet_w, fnet_b)], axis=-1) @ w_out
    if not need_ctx:
        return out_l, None
    out_c = jnp.concatenate([ssd_c, s5_c, fnet_branch(ufc, gfc, fnet_w, fnet_b)], axis=-1) @ w_out
    return out_l, out_c


def setup_inputs(seed: int = 0) -> dict:
    key = jax.random.key(seed)
    ks = jax.random.split(key, 32)
    f32 = jnp.float32
    L = DEPTH

    def nrm(k, shape, scale):
        return jax.random.normal(k, shape, f32) * scale

    def log_uniform(k, shape, lo, hi):
        return jax.random.uniform(k, shape, f32, np.log(lo), np.log(hi))

    dt0 = jnp.exp(log_uniform(ks[11], (L, 2, SSD_HEADS), 1e-3, 1e-1))
    lam_im = jnp.broadcast_to(jnp.pi * jnp.arange(S5_STATE, dtype=f32), (L, 2, S5_GROUPS, S5_STATE))
    return {
        'x': nrm(ks[0], (BATCH, SEQ, D_MODEL), 1.0),
        'c': nrm(ks[1], (BATCH, D_MODEL), 1.0),
        'ctx': nrm(ks[2], (BATCH, CTX_LEN, D_MODEL), 1.0),
        'c_ctx': nrm(ks[3], (D_MODEL,), 1.0),
        'w_mod': nrm(ks[4], (L, D_MODEL, 3 * D_MODEL), 0.5 * D_MODEL ** -0.5),
        'b_mod': nrm(ks[5], (L, 3 * D_MODEL), 0.02),
        'g_pre': 1.0 + nrm(ks[6], (L, D_MODEL), 0.02),
        'g_post': 1.0 + nrm(ks[7], (L, D_MODEL), 0.02),
        'w_in': nrm(ks[8], (L, D_MODEL, D_IN_PROJ), D_MODEL ** -0.5),
        'conv_w': nrm(ks[9], (L, CONV_K, CONV_K, D_CONV_CH), 1.0 / CONV_K),
        'conv_b': nrm(ks[10], (L, D_CONV_CH), 0.02),
        'dt_bias': dt0 + jnp.log(-jnp.expm1(-dt0)),
        'a_log': jnp.log(jax.random.uniform(ks[12], (L, 2, SSD_HEADS), f32, 1.0, 16.0)),
        'd_ssd': 1.0 + nrm(ks[13], (L, SSD_HEADS), 0.02),
        'g_ssd_norm': 1.0 + nrm(ks[14], (L, D_SSD), 0.02),
        's5_lambda_re': -0.5 + nrm(ks[15], (L, 2, S5_GROUPS, S5_STATE), 0.01),
        's5_lambda_im': lam_im,
        's5_log_step': log_uniform(ks[16], (L, 2, S5_GROUPS), 1e-3, 1e-1),
        's5_b_re': nrm(ks[17], (L, 2, S5_GROUPS, S5_STATE, S5_GROUP), (2.0 * S5_GROUP) ** -0.5),
        's5_b_im': nrm(ks[18], (L, 2, S5_GROUPS, S5_STATE, S5_GROUP), (2.0 * S5_GROUP) ** -0.5),
        's5_c_re': nrm(ks[19], (L, 2, S5_GROUPS, S5_GROUP, S5_STATE), (2.0 * S5_STATE) ** -0.5),
        's5_c_im': nrm(ks[20], (L, 2, S5_GROUPS, S5_GROUP, S5_STATE), (2.0 * S5_STATE) ** -0.5),
        's5_d': nrm(ks[21], (L, D_S5), 1.0),
        's5_w_glu': nrm(ks[22], (L, D_S5, D_S5), D_S5 ** -0.5),
        's5_b_glu': nrm(ks[23], (L, D_S5), 0.02),
        'fnet_w': nrm(ks[24], (L, D_FNET, D_FNET), D_FNET ** -0.5),
        'fnet_b': nrm(ks[25], (L, D_FNET), 0.02),
        'w_out': nrm(ks[26], (L, D_MIX, D_MODEL), D_MIX ** -0.5),
    }


def reference(x, c, ctx, c_ctx, w_mod, b_mod, g_pre, g_post, w_in, conv_w, conv_b, dt_bias, a_log,
              d_ssd, g_ssd_norm, s5_lambda_re, s5_lambda_im, s5_log_step, s5_b_re, s5_b_im,
              s5_c_re, s5_c_im, s5_d, s5_w_glu, s5_b_glu, fnet_w, fnet_b, w_out):
    xl, xc = x, ctx
    for i in range(DEPTH):
        need_ctx = i < DEPTH - 1
        mod_l = jax.nn.silu(c) @ w_mod[i] + b_mod[i]
        mod_c = jax.nn.silu(c_ctx) @ w_mod[i] + b_mod[i]
        shift_l, scale_l, gate_l = jnp.split(mod_l[:, None, :], 3, axis=-1)
        shift_c, scale_c, gate_c = jnp.split(mod_c, 3, axis=-1)
        hl = rmsnorm(xl, g_pre[i]) * (1.0 + scale_l) + shift_l
        hc = rmsnorm(xc, g_pre[i]) * (1.0 + scale_c) + shift_c
        yl, yc = hybrid_mixer(hl, hc, w_in[i], conv_w[i], conv_b[i], dt_bias[i], a_log[i], d_ssd[i],
                              g_ssd_norm[i], s5_lambda_re[i], s5_lambda_im[i], s5_log_step[i],
                              s5_b_re[i], s5_b_im[i], s5_c_re[i], s5_c_im[i], s5_d[i], s5_w_glu[i],
                              s5_b_glu[i], fnet_w[i], fnet_b[i], w_out[i], need_ctx)
        xl = xl + gate_l * rmsnorm(yl, g_post[i])
        if need_ctx:
            xc = xc + gate_c * rmsnorm(yc, g_post[i])
    return xl
```

```python
import functools

import numpy as np
import jax
import jax.numpy as jnp
from jax import lax
from jax.experimental import pallas as pl
from jax.experimental.pallas import tpu as pltpu

f32 = jnp.float32
bf16 = jnp.bfloat16

D_MODEL = 1024
SEQ = 4096
CTX = 256
NROW = SEQ + CTX
DEPTH = 4
GRID_W = 64
D_SSD = 1024
HEADDIM = 64
HEADS = 16
GROUPS = 2
HPG = HEADS // GROUPS
NSTATE = 128
CHUNK = 128
NCHUNK = NROW // CHUNK
NCHUNK_LAT = SEQ // CHUNK
D_XBC = D_SSD + 2 * GROUPS * NSTATE
D_S5 = 512
S5_G = 32
S5_C = 16
S5_P = 64
S5_T = 16
S5_NCH = NROW // S5_T
S5_NCH_LAT = SEQ // S5_T
S5_NCH_CTX = CTX // S5_T
D_FNET = 512
FN_HEADS = 4
FN_HD = 128
DT_PAD = 128
D_PACK = D_SSD + D_XBC + 4 * 512 + DT_PAD
EPS = 1e-6
TM = 256
NT = NROW // TM
VMEM_LIMIT = 56 * 1024 * 1024


def _sigmoid(v):
    return 1.0 / (1.0 + jnp.exp(-v))


def _silu(v):
    return v * _sigmoid(v)


def _split2(v):
    hi = v.astype(bf16)
    lo = (v - hi.astype(f32)).astype(bf16)
    return hi, lo


def _split3(v):
    hi = v.astype(bf16)
    r = v - hi.astype(f32)
    mid = r.astype(bf16)
    lo = (r - mid.astype(f32)).astype(bf16)
    return hi, mid, lo


def _dot(a, b):
    return jnp.dot(a, b, preferred_element_type=f32)


def _dot_nt(a, b):
    return lax.dot_general(a, b, (((1,), (1,)), ((), ())), preferred_element_type=f32)


def _mod_kernel(c_ref, w_ref, b_ref, o_ref):
    s = _silu(c_ref[...])
    o_ref[0] = _dot(s.astype(bf16), w_ref[0].astype(bf16)) + b_ref[0]


def _mod_all(c8, w_mod, b_mod):
    depth = w_mod.shape[0]
    return pl.pallas_call(
        _mod_kernel,
        out_shape=jax.ShapeDtypeStruct((depth, 8, 3 * D_MODEL), f32),
        grid=(depth, 3),
        in_specs=[
            pl.BlockSpec((8, D_MODEL), lambda l, n: (0, 0)),
            pl.BlockSpec((1, D_MODEL, D_MODEL), lambda l, n: (l, 0, n)),
            pl.BlockSpec((1, 1, D_MODEL), lambda l, n: (l, 0, n)),
        ],
        out_specs=pl.BlockSpec((1, 8, D_MODEL), lambda l, n: (l, 0, n)),
        name="mod",
    )(c8, w_mod, b_mod)


_PROJ_WIDTHS = (D_SSD, D_XBC, 512, 512, 512, 512, DT_PAD)


def _inproj_kernel(x_ref, mod_ref, g_ref, w_ref, *out_refs):
    b = pl.program_id(0)
    j = pl.program_id(1)
    row = jnp.where(j == NT - 1, pl.num_programs(0), b)
    m = mod_ref[pl.ds(row, 1), :]
    shift = m[:, 0:D_MODEL]
    scale = m[:, D_MODEL:2 * D_MODEL]
    x = x_ref[0]
    ms = jnp.mean(x * x, axis=-1, keepdims=True)
    xn = x * lax.rsqrt(ms + EPS) * g_ref[...]
    h = (xn * (1.0 + scale) + shift).astype(bf16)
    off = 0
    for ref, width in zip(out_refs, _PROJ_WIDTHS):
        ref[0] = _dot(h, w_ref[:, off:off + width])
        off += width


def _inproj(x, mod_l, g_pre, w_pack):
    nb = x.shape[0]
    outs = tuple(jax.ShapeDtypeStruct((nb, NROW, w), f32) for w in _PROJ_WIDTHS)
    return pl.pallas_call(
        _inproj_kernel,
        out_shape=outs,
        grid=(nb, NT),
        in_specs=[
            pl.BlockSpec((1, TM, D_MODEL), lambda b, j: (b, j, 0)),
            pl.BlockSpec((8, 3 * D_MODEL), lambda b, j: (0, 0)),
            pl.BlockSpec((1, D_MODEL), lambda b, j: (0, 0)),
            pl.BlockSpec((D_MODEL, D_PACK), lambda b, j: (0, 0)),
        ],
        out_specs=tuple(pl.BlockSpec((1, TM, w), lambda b, j: (b, j, 0)) for w in _PROJ_WIDTHS),
        compiler_params=pltpu.CompilerParams(vmem_limit_bytes=VMEM_LIMIT),
        name="inproj",
    )(x, mod_l, g_pre, w_pack)


def _conv_kernel(x_ref, w_ref, b_ref, o_ref, pad_ref):
    zero = jnp.zeros((GRID_W, 128), f32)
    pad_ref[0:GRID_W, :] = zero
    pad_ref[GRID_W + SEQ:2 * GRID_W + SEQ, :] = zero
    pad_ref[GRID_W:GRID_W + SEQ, :] = x_ref[0, 0:SEQ, :]
    w = w_ref[...]
    bias = b_ref[...]
    rid = lax.broadcasted_iota(jnp.int32, (GRID_W, 128), 0)
    first = rid == 0
    last = rid == GRID_W - 1

    def body(r, carry):
        acc = jnp.zeros((GRID_W, 128), f32) + bias
        for i in range(3):
            start = pl.multiple_of((r + i) * GRID_W, GRID_W)
            v = pad_ref[pl.ds(start, GRID_W), :]
            vm = jnp.where(first, 0.0, pltpu.roll(v, 1, 0))
            vp = jnp.where(last, 0.0, pltpu.roll(v, GRID_W - 1, 0))
            acc = acc + w[3 * i:3 * i + 1] * vm + w[3 * i + 1:3 * i + 2] * v + w[3 * i + 2:3 * i + 3] * vp
        o_ref[0, pl.ds(pl.multiple_of(r * GRID_W, GRID_W), GRID_W), :] = _silu(acc)
        return carry

    lax.fori_loop(0, SEQ // GRID_W, body, 0)

    xc = x_ref[0, SEQ:NROW, :]
    cid = lax.broadcasted_iota(jnp.int32, (CTX, 128), 0)
    vm = jnp.where(cid == 0, 0.0, pltpu.roll(xc, 1, 0))
    vp = jnp.where(cid == CTX - 1, 0.0, pltpu.roll(xc, CTX - 1, 0))
    acc = bias + w[3:4] * vm + w[4:5] * xc + w[5:6] * vp
    o_ref[0, SEQ:NROW, :] = _silu(acc)


def _conv(xbc, w9, cb):
    nb = xbc.shape[0]
    return pl.pallas_call(
        _conv_kernel,
        out_shape=jax.ShapeDtypeStruct((nb, NROW, D_XBC), f32),
        grid=(nb, D_XBC // 128),
        in_specs=[
            pl.BlockSpec((1, NROW, 128), lambda b, c: (b, 0, c)),
            pl.BlockSpec((9, 128), lambda b, c: (0, c)),
            pl.BlockSpec((1, 128), lambda b, c: (0, c)),
        ],
        out_specs=pl.BlockSpec((1, NROW, 128), lambda b, c: (b, 0, c)),
        scratch_shapes=[pltpu.VMEM((SEQ + 2 * GRID_W, 128), f32)],
        name="conv",
    )(xbc, w9, cb)


def _ssd_kernel(xf_ref, bf_ref, cf_ref, dtf_ref, xb_ref, bb_ref, cb_ref, dtb_ref,
                dtbias_ref, alog_ref, e_ref, eye_ref, yf_ref, yb_ref, h_ref):
    s = pl.program_id(1)

    @pl.when(s == 0)
    def _():
        h_ref[...] = jnp.zeros(h_ref.shape, f32)

    expand = e_ref[...]
    eye = eye_ref[...]
    ii = lax.broadcasted_iota(jnp.int32, (CHUNK, CHUNK), 0)
    jj = lax.broadcasted_iota(jnp.int32, (CHUNK, CHUNK), 1)
    dirs = ((xf_ref, bf_ref, cf_ref, dtf_ref, yf_ref), (xb_ref, bb_ref, cb_ref, dtb_ref, yb_ref))
    for d, (x_ref, b_ref, c_ref, dt_ref, y_ref) in enumerate(dirs):
        keep = (ii >= jj) if d == 0 else (ii <= jj)
        tri = keep.astype(bf16)
        x = x_ref[0]
        bm = b_ref[0]
        cm = c_ref[0]
        v = dt_ref[0][:, HEADS * d:HEADS * (d + 1)] + dtbias_ref[d:d + 1, :]
        dt = jnp.maximum(v, 0.0) + jnp.log1p(jnp.exp(-jnp.abs(v)))
        a = dt * (-jnp.exp(alog_ref[d:d + 1, :]))
        a3 = _split3(a)
        acum = _dot(tri, a3[0]) + _dot(tri, a3[1]) + _dot(tri, a3[2])
        total = acum[CHUNK - 1:CHUNK] if d == 0 else acum[0:1]
        c3 = _split3(acum)
        acum_t = _dot_nt(eye, c3[0]) + _dot_nt(eye, c3[1]) + _dot_nt(eye, c3[2])
        stack = jnp.concatenate(
            [dt, jnp.exp(acum), jnp.exp(total - acum), jnp.broadcast_to(jnp.exp(total), (8, HEADS))], axis=0)
        s_hi, s_lo = _split2(stack)
        ex = _dot(s_hi, expand) + _dot(s_lo, expand)
        dt_e = ex[0:CHUNK]
        ea_e = ex[CHUNK:2 * CHUNK]
        dte_e = ex[2 * CHUNK:3 * CHUNK]
        cd_e = ex[3 * CHUNK:3 * CHUNK + 1]
        xdt = x * dt_e
        xdt_b = xdt.astype(bf16)
        xw_b = (xdt * dte_e).astype(bf16)
        for g in range(GROUPS):
            bg = bm[:, NSTATE * g:NSTATE * (g + 1)]
            cg_b = cm[:, NSTATE * g:NSTATE * (g + 1)].astype(bf16)
            cbm = _dot_nt(cg_b, bg.astype(bf16))
            hprev = h_ref[d, g]
            gl = slice(HPG * HEADDIM * g, HPG * HEADDIM * (g + 1))
            yoff = _dot(cg_b, hprev.astype(bf16)) * ea_e[:, gl]
            for hh in range(HPG):
                h = HPG * g + hh
                seg = acum[:, h:h + 1] - acum_t[h:h + 1, :]
                dec = jnp.where(keep, jnp.exp(jnp.where(keep, seg, 0.0)), 0.0)
                gmat = (cbm * dec).astype(bf16)
                yd = _dot(gmat, xdt_b[:, HEADDIM * h:HEADDIM * (h + 1)])
                y_ref[0, :, HEADDIM * h:HEADDIM * (h + 1)] = yd + yoff[:, HEADDIM * hh:HEADDIM * (hh + 1)]
            st = _dot(bg.T.astype(bf16), xw_b[:, gl])
            h_ref[d, g] = hprev * cd_e[:, gl] + st


def _ssd(act, dtp, dt_bias, a_log, expand, eye):
    nb = act.shape[0]

    def fwd_chunk(s):
        return jnp.where(s < 2, NCHUNK_LAT + s, s - 2)

    def bwd_chunk(s):
        return NCHUNK - 1 - s

    def specs(chunk_fn):
        return [
            pl.BlockSpec((1, CHUNK, D_SSD), lambda b, s: (b, chunk_fn(s), 0)),
            pl.BlockSpec((1, CHUNK, 2 * NSTATE), lambda b, s: (b, chunk_fn(s), D_SSD // (2 * NSTATE))),
            pl.BlockSpec((1, CHUNK, 2 * NSTATE), lambda b, s: (b, chunk_fn(s), D_SSD // (2 * NSTATE) + 1)),
            pl.BlockSpec((1, CHUNK, DT_PAD), lambda b, s: (b, chunk_fn(s), 0)),
        ]

    out = jax.ShapeDtypeStruct((nb, NROW, D_SSD), f32)
    return pl.pallas_call(
        _ssd_kernel,
        out_shape=(out, out),
        grid=(nb, NCHUNK),
        in_specs=specs(fwd_chunk) + specs(bwd_chunk) + [
            pl.BlockSpec((2, HEADS), lambda b, s: (0, 0)),
            pl.BlockSpec((2, HEADS), lambda b, s: (0, 0)),
            pl.BlockSpec((HEADS, D_SSD), lambda b, s: (0, 0)),
            pl.BlockSpec((HEADS, HEADS), lambda b, s: (0, 0)),
        ],
        out_specs=(
            pl.BlockSpec((1, CHUNK, D_SSD), lambda b, s: (b, fwd_chunk(s), 0)),
            pl.BlockSpec((1, CHUNK, D_SSD), lambda b, s: (b, bwd_chunk(s), 0)),
        ),
        scratch_shapes=[pltpu.VMEM((2, GROUPS, NSTATE, HPG * HEADDIM), f32)],
        compiler_params=pltpu.CompilerParams(dimension_semantics=("arbitrary", "arbitrary")),
        name="ssd",
    )(act, act, act, dtp, act, act, act, dtp, dt_bias, a_log, expand, eye)


def _s5_matrices(lam_re, lam_im, log_step, b_re, b_im, c_re, c_im):
    hp = lax.Precision.HIGHEST
    t = S5_T
    lr = lam_re.astype(f32)
    li = lam_im.astype(f32)
    step = jnp.exp(log_step.astype(f32))[..., None]
    mag = jnp.exp(lr * step)
    ar = mag * jnp.cos(li * step)
    ai = mag * jnp.sin(li * step)
    den = lr * lr + li * li
    fr = ((ar - 1.0) * lr + ai * li) / den
    fi = (ai * lr - (ar - 1.0) * li) / den
    br = b_re.astype(f32)
    bi = b_im.astype(f32)
    bbr = fr[..., None] * br - fi[..., None] * bi
    bbi = fr[..., None] * bi + fi[..., None] * br
    taus = jnp.arange(t + 1, dtype=f32)[:, None]
    pmag = jnp.exp(taus * (lr * step)[:, :, None, :])
    pang = taus * (li * step)[:, :, None, :]
    pwr = pmag * jnp.cos(pang)
    pwi = pmag * jnp.sin(pang)
    cr = c_re.astype(f32)[:, :, None]
    ci = c_im.astype(f32)[:, :, None]
    dr = cr * pwr[:, :, :, None, :] - ci * pwi[:, :, :, None, :]
    di = cr * pwi[:, :, :, None, :] + ci * pwr[:, :, :, None, :]
    kern = (jnp.einsum('dgtkp,dgpj->dgtkj', dr, bbr, precision=hp)
            - jnp.einsum('dgtkp,dgpj->dgtkj', di, bbi, precision=hp))
    tt = jnp.arange(t)[:, None]
    ss = jnp.arange(t)[None, :]
    kf = kern[0][:, jnp.clip(tt - ss, 0, t)]
    kb = kern[1][:, jnp.clip(ss - tt, 0, t)]
    m = (jnp.where((tt >= ss)[None, :, :, None, None], kf, 0.0)
         + jnp.where((ss >= tt)[None, :, :, None, None], kb, 0.0))
    m_t = m.transpose(0, 2, 4, 1, 3).reshape(S5_G, t * S5_C, t * S5_C)
    ef = jnp.arange(t - 1, -1, -1)
    eb = jnp.arange(t)

    def state_map(d, exps):
        pr = pwr[d][:, exps]
        pi = pwi[d][:, exps]
        re = pr[:, :, None, :] * bbr[d].transpose(0, 2, 1)[:, None] - pi[:, :, None, :] * bbi[d].transpose(0, 2, 1)[:, None]
        im = pr[:, :, None, :] * bbi[d].transpose(0, 2, 1)[:, None] + pi[:, :, None, :] * bbr[d].transpose(0, 2, 1)[:, None]
        return re.reshape(S5_G, t * S5_C, S5_P), im.reshape(S5_G, t * S5_C, S5_P)

    pfr, pfi = state_map(0, ef)
    pbr, pbi = state_map(1, eb)

    def read_map(d, exps):
        qr = dr[d][:, exps].transpose(0, 3, 1, 2).reshape(S5_G, S5_P, t * S5_C)
        qi = -di[d][:, exps].transpose(0, 3, 1, 2).reshape(S5_G, S5_P, t * S5_C)
        return qr, qi

    qfr, qfi = read_map(0, jnp.arange(1, t + 1))
    qbr, qbi = read_map(1, jnp.arange(t, 0, -1))

    npair = S5_G // 2
    w = t * S5_C

    def pair_cols(mat):
        rows = mat.shape[1]
        mp = mat.reshape(npair, 2, rows, S5_P)
        z = jnp.zeros_like(mp[:, 0])
        top = jnp.concatenate([mp[:, 0], z], axis=-1)
        bot = jnp.concatenate([z, mp[:, 1]], axis=-1)
        return jnp.concatenate([top, bot], axis=1)

    mp = m_t.reshape(npair, 2, w, w)
    zz = jnp.zeros_like(mp[:, 0])
    m_pair = jnp.concatenate([jnp.concatenate([mp[:, 0], zz], axis=-1),
                              jnp.concatenate([zz, mp[:, 1]], axis=-1)], axis=1)
    w1 = jnp.concatenate([m_pair, pair_cols(pfr), pair_cols(pfi), pair_cols(pbr), pair_cols(pbi)], axis=-1)

    def pair_rows(mat):
        mp2 = mat.reshape(npair, 2, S5_P, w)
        z = jnp.zeros_like(mp2[:, 0])
        top = jnp.concatenate([mp2[:, 0], z], axis=-1)
        bot = jnp.concatenate([z, mp2[:, 1]], axis=-1)
        return jnp.concatenate([top, bot], axis=1)

    w2 = jnp.concatenate([pair_rows(qfr), pair_rows(qfi), pair_rows(qbr), pair_rows(qbi)], axis=1)
    tr = jnp.repeat(pwr[:, :, t, :].reshape(2, 1, S5_G * S5_P), 4, axis=1).reshape(8, S5_G * S5_P)
    ti = jnp.repeat(pwi[:, :, t, :].reshape(2, 1, S5_G * S5_P), 4, axis=1).reshape(8, S5_G * S5_P)
    return w1.astype(bf16), w2.astype(bf16), tr, ti


def _s5a_kernel(u_ref, w_ref, yi_ref, sr_ref, si_ref, *, nb):
    u = jnp.concatenate([u_ref[0], u_ref[1]], axis=-1).astype(bf16)
    r = _dot(u, w_ref[0])
    yi_ref[0] = r[:, 0:512]
    for d in range(2):
        for b in range(nb):
            rows = slice(b * S5_NCH, (b + 1) * S5_NCH)
            k = 4 * d + b
            sr_ref[pl.ds(k, S5_NCH, stride=8), :] = r[rows, 512 + 256 * d:512 + 256 * d + 128]
            si_ref[pl.ds(k, S5_NCH, stride=8), :] = r[rows, 640 + 256 * d:640 + 256 * d + 128]


def _s5a(u_t, w1, nb):
    npair = S5_G // 2
    rows = nb * S5_NCH
    return pl.pallas_call(
        functools.partial(_s5a_kernel, nb=nb),
        out_shape=(
            jax.ShapeDtypeStruct((npair, rows, 512), f32),
            jax.ShapeDtypeStruct((S5_NCH * 8, S5_G * S5_P), f32),
            jax.ShapeDtypeStruct((S5_NCH * 8, S5_G * S5_P), f32),
        ),
        grid=(npair,),
        in_specs=[
            pl.BlockSpec((2, rows, 256), lambda p: (p, 0, 0)),
            pl.BlockSpec((1, 512, 1024), lambda p: (p, 0, 0)),
        ],
        out_specs=(
            pl.BlockSpec((1, rows, 512), lambda p: (p, 0, 0)),
            pl.BlockSpec((S5_NCH * 8, 128), lambda p: (0, p)),
            pl.BlockSpec((S5_NCH * 8, 128), lambda p: (0, p)),
        ),
        name="s5_chunk",
    )(u_t, w1)


def _s5scan_kernel(sr_ref, si_ref, tr_ref, ti_ref, hfr_ref, hfi_ref, hbr_ref, hbi_ref):
    tr = tr_ref[...]
    ti = ti_ref[...]
    lanes = tr.shape[1]
    is_fwd = lax.broadcasted_iota(jnp.int32, (8, lanes), 0) < 4

    def body(s, carry):
        hr, hi = carry
        cf = jnp.where(s < S5_NCH_CTX, S5_NCH_LAT + s, s - S5_NCH_CTX)
        cb = S5_NCH - 1 - s
        rf = pl.multiple_of(cf * 8, 8)
        rb = pl.multiple_of(cb * 8, 8)
        xr = jnp.where(is_fwd, sr_ref[pl.ds(rf, 8), :], sr_ref[pl.ds(rb, 8), :])
        xi = jnp.where(is_fwd, si_ref[pl.ds(rf, 8), :], si_ref[pl.ds(rb, 8), :])
        hfr_ref[pl.ds(rf, 8), :] = hr
        hfi_ref[pl.ds(rf, 8), :] = hi
        hbr_ref[pl.ds(rb, 8), :] = hr
        hbi_ref[pl.ds(rb, 8), :] = hi
        return tr * hr - ti * hi + xr, tr * hi + ti * hr + xi

    z = jnp.zeros((8, lanes), f32)
    lax.fori_loop(0, S5_NCH, body, (z, z))


def _s5scan(sr, si, tr, ti):
    lanes = 256
    nblk = (S5_G * S5_P) // lanes
    big = pl.BlockSpec((S5_NCH * 8, lanes), lambda i: (0, i))
    small = pl.BlockSpec((8, lanes), lambda i: (0, i))
    out = jax.ShapeDtypeStruct((S5_NCH * 8, S5_G * S5_P), f32)
    return pl.pallas_call(
        _s5scan_kernel,
        out_shape=(out, out, out, out),
        grid=(nblk,),
        in_specs=[big, big, small, small],
        out_specs=(big, big, big, big),
        compiler_params=pltpu.CompilerParams(vmem_limit_bytes=VMEM_LIMIT),
        name="s5_scan",
    )(sr, si, tr, ti)


def _s5c_kernel(yi_ref, hfr_ref, hfi_ref, hbr_ref, hbi_ref, w_ref, y_ref, *, nb):
    parts = []
    for ref, d in ((hfr_ref, 0), (hfi_ref, 0), (hbr_ref, 1), (hbi_ref, 1)):
        rows = [ref[pl.ds(4 * d + b, S5_NCH, stride=8), :] for b in range(nb)]
        parts.append(jnp.concatenate(rows, axis=0))
    hin = jnp.concatenate(parts, axis=-1).astype(bf16)
    y_ref[0] = yi_ref[0] + _dot(hin, w_ref[0])


def _s5c(yi, hs, w2, nb):
    npair = S5_G // 2
    rows = nb * S5_NCH
    hspec = pl.BlockSpec((S5_NCH * 8, 128), lambda p: (0, p))
    return pl.pallas_call(
        functools.partial(_s5c_kernel, nb=nb),
        out_shape=jax.ShapeDtypeStruct((npair, rows, 512), f32),
        grid=(npair,),
        in_specs=[pl.BlockSpec((1, rows, 512), lambda p: (p, 0, 0)), hspec, hspec, hspec, hspec,
                  pl.BlockSpec((1, 512, 512), lambda p: (p, 0, 0))],
        out_specs=pl.BlockSpec((1, rows, 512), lambda p: (p, 0, 0)),
        name="s5_read",
    )(yi, *hs, w2)


def _s5_mix(u, mats):
    nb = u.shape[0]
    w1, w2, tr, ti = mats
    u_t = u.reshape(nb, S5_NCH, S5_T, S5_G, S5_C).transpose(3, 0, 1, 2, 4).reshape(S5_G, nb * S5_NCH, S5_T * S5_C)
    yi, sr, si = _s5a(u_t, w1, nb)
    hs = _s5scan(sr, si, tr, ti)
    y_t = _s5c(yi, hs, w2, nb)
    y = y_t.reshape(S5_G // 2, nb, S5_NCH, 2, S5_T, S5_C).transpose(1, 2, 4, 0, 3, 5)
    return y.reshape(nb, NROW, D_S5)


def _fnet_consts():
    n1 = GRID_W
    k = np.arange(n1)
    ang = 2.0 * np.pi * np.outer(k, k) / n1
    c64, s64 = np.cos(ang), np.sin(ang)
    fa = np.concatenate([c64, -s64], axis=0) / n1
    gb = np.block([[c64, s64], [-s64, c64]])
    cc = np.arange(n1)[:, None, None]
    kr = np.arange(n1)[None, :, None]
    tw = 2.0 * np.pi * cc * kr / SEQ * np.ones((1, 1, FN_HD))
    twc = np.cos(tw).reshape(n1 * n1, FN_HD)
    tws = np.sin(tw).reshape(n1 * n1, FN_HD)
    j = np.arange(FN_HD)
    angc = 2.0 * np.pi * np.outer(j, j) / FN_HD
    ch = np.concatenate([np.cos(angc), np.sin(angc)], axis=0) / np.sqrt(FN_HD)
    t = np.arange(CTX)
    angt = 2.0 * np.pi * np.outer(t, t) / CTX
    fc = np.concatenate([np.cos(angt), -np.sin(angt)], axis=0) / np.sqrt(CTX)
    as32 = lambda a: jnp.asarray(a.astype(np.float32))
    return as32(fa), as32(gb), as32(twc), as32(tws), as32(ch), as32(fc)


def _fnet_kernel(x_ref, fa_ref, gb_ref, twc_ref, tws_ref, ch_ref, fc_ref, o_ref, z_ref):
    n1 = GRID_W
    fa = fa_ref[...]
    gb = gb_ref[...]
    chm = ch_ref[...]

    def stage_a(c, carry):
        xc = x_ref[0, pl.ds(c, n1, stride=n1), :].astype(bf16)
        z = _dot(fa, xc)
        zr = z[0:n1]
        zi = z[n1:2 * n1]
        row = pl.multiple_of(c * n1, n1)
        cs = twc_ref[pl.ds(row, n1), :]
        sn = tws_ref[pl.ds(row, n1), :]
        z_ref[pl.ds(c, n1, stride=2 * n1), :] = zr * cs + zi * sn
        z_ref[pl.ds(n1 + c, n1, stride=2 * n1), :] = zi * cs - zr * sn
        return carry

    lax.fori_loop(0, n1, stage_a, 0)

    def stage_b(kr, carry):
        slab = z_ref[pl.ds(pl.multiple_of(kr * 2 * n1, 2 * n1), 2 * n1), :].astype(bf16)
        o = _dot(gb, slab)
        v = jnp.concatenate([o[0:n1], o[n1:2 * n1]], axis=-1).astype(bf16)
        o_ref[0, pl.ds(kr, n1, stride=n1), :] = _dot(v, chm)
        return carry

    lax.fori_loop(0, n1, stage_b, 0)

    xc = x_ref[0, SEQ:NROW, :].astype(bf16)
    vc = _dot(fc_ref[...], xc)
    v = jnp.concatenate([vc[0:CTX], vc[CTX:2 * CTX]], axis=-1).astype(bf16)
    o_ref[0, SEQ:NROW, :] = _dot(v, chm)


def _fnet(fu, consts):
    nb = fu.shape[0]
    fa, gb, twc, tws, ch, fc = consts
    full = lambda a: pl.BlockSpec(a.shape, lambda b, h: (0, 0))
    return pl.pallas_call(
        _fnet_kernel,
        out_shape=jax.ShapeDtypeStruct((nb, NROW, D_FNET), f32),
        grid=(nb, FN_HEADS),
        in_specs=[pl.BlockSpec((1, NROW, FN_HD), lambda b, h: (b, 0, h)),
                  full(fa), full(gb), full(twc), full(tws), full(ch), full(fc)],
        out_specs=pl.BlockSpec((1, NROW, FN_HD), lambda b, h: (b, 0, h)),
        scratch_shapes=[pltpu.VMEM((2 * GRID_W * GRID_W, FN_HD), f32)],
        name="fnet",
    )(fu, fa, gb, twc, tws, ch, fc)


def _out_kernel(x_ref, xs_ref, yf_ref, yb_ref, z_ref, ys_ref, u_ref, gs_ref, sp_ref, fg_ref,
                mod_ref, dsk_ref, gn_ref, sd_ref, wg_ref, bg_ref, fw_ref, fb_ref, wo_ref, gp_ref, o_ref):
    b = pl.program_id(0)
    j = pl.program_id(1)
    row = jnp.where(j == NT - 1, pl.num_programs(0), b)
    gate = mod_ref[pl.ds(row, 1), 2 * D_MODEL:3 * D_MODEL]
    y = dsk_ref[...] * xs_ref[0] + yf_ref[0] + yb_ref[0]
    v = y * _silu(z_ref[0])
    gw = D_SSD // GROUPS
    acc = None
    for g in range(GROUPS):
        vg = v[:, gw * g:gw * (g + 1)]
        vg = vg * lax.rsqrt(jnp.mean(vg * vg, axis=-1, keepdims=True) + EPS) * gn_ref[:, gw * g:gw * (g + 1)]
        part = _dot(vg.astype(bf16), wo_ref[gw * g:gw * (g + 1), :])
        acc = part if acc is None else acc + part
    ys = ys_ref[0] + sd_ref[...] * u_ref[0]
    gl = 0.5 * ys * (1.0 + jnp.tanh(0.7978845608028654 * (ys + 0.044715 * ys * ys * ys)))
    s5o = gl * _sigmoid(_dot(gl.astype(bf16), wg_ref[...]) + bg_ref[...]) * _silu(gs_ref[0])
    acc = acc + _dot(s5o.astype(bf16), wo_ref[D_SSD:D_SSD + D_S5, :])
    fo = (_dot(sp_ref[0].astype(bf16), fw_ref[...]) + fb_ref[...]) * _silu(fg_ref[0])
    acc = acc + _dot(fo.astype(bf16), wo_ref[D_SSD + D_S5:D_SSD + D_S5 + D_FNET, :])
    yn = acc * lax.rsqrt(jnp.mean(acc * acc, axis=-1, keepdims=True) + EPS) * gp_ref[...]
    o_ref[0] = x_ref[0] + gate * yn


def _out(x, act, yf, yb, z, ys, u, gs, sp, fg, mod_l, dsk, gn, sd, wg, bg, fw, fb, wo, gp):
    nb = x.shape[0]
    row = lambda w, c=0: pl.BlockSpec((1, TM, w), lambda b, j: (b, j, c))
    full = lambda a: pl.BlockSpec(a.shape, lambda b, j: (0, 0))
    return pl.pallas_call(
        _out_kernel,
        out_shape=jax.ShapeDtypeStruct((nb, NROW, D_MODEL), f32),
        grid=(nb, NT),
        in_specs=[row(D_MODEL), row(D_SSD), row(D_SSD), row(D_SSD), row(D_SSD), row(512), row(512), row(512),
                  row(512), row(512), full(mod_l), full(dsk), full(gn), full(sd), full(wg), full(bg),
                  full(fw), full(fb), full(wo), full(gp)],
        out_specs=row(D_MODEL),
        compiler_params=pltpu.CompilerParams(vmem_limit_bytes=VMEM_LIMIT),
        name="out",
    )(x, act, yf, yb, z, ys, u, gs, sp, fg, mod_l, dsk, gn, sd, wg, bg, fw, fb, wo, gp)


def _pack_w_in(w):
    z, xbc, dt, su, sg, fu, fg = jnp.split(w, (1024, 2560, 2592, 3104, 3616, 4128), axis=-1)
    dt = jnp.pad(dt, ((0, 0), (0, DT_PAD - 2 * HEADS)))
    return jnp.concatenate([z, xbc, su, sg, fu, fg, dt], axis=-1).astype(bf16)


def kernel(x, c, ctx, c_ctx, w_mod, b_mod, g_pre, g_post, w_in, conv_w, conv_b, dt_bias, a_log, d_ssd, g_ssd_norm, s5_lambda_re, s5_lambda_im, s5_log_step, s5_b_re, s5_b_im, s5_c_re, s5_c_im, s5_d, s5_w_glu, s5_b_glu, fnet_w, fnet_b, w_out):
    nb = x.shape[0]
    depth = w_in.shape[0]
    xs = jnp.concatenate([x, ctx], axis=1)
    c8 = jnp.concatenate([c, c_ctx[None, :], jnp.zeros((8 - nb - 1, D_MODEL), f32)], axis=0)
    mod = _mod_all(c8, w_mod, b_mod[:, None, :])
    expand = jnp.repeat(jnp.eye(HEADS, dtype=f32), HEADDIM, axis=1).astype(bf16)
    eye = jnp.eye(HEADS, dtype=f32).astype(bf16)
    fconsts = tuple(a.astype(bf16) if i in (0, 1, 4, 5) else a for i, a in enumerate(_fnet_consts()))
    for i in range(depth):
        z, xbc, u, gs, fu, fg, dtp = _inproj(xs, mod[i], g_pre[i][None, :], _pack_w_in(w_in[i]))
        act = _conv(xbc, conv_w[i].reshape(9, D_XBC), conv_b[i][None, :])
        yf, yb = _ssd(act, dtp, dt_bias[i], a_log[i], expand, eye)
        mats = _s5_matrices(s5_lambda_re[i], s5_lambda_im[i], s5_log_step[i], s5_b_re[i], s5_b_im[i],
                            s5_c_re[i], s5_c_im[i])
        ys = _s5_mix(u, mats)
        sp = _fnet(fu, fconsts)
        xs = _out(xs, act, yf, yb, z, ys, u, gs, sp, fg, mod[i],
                  jnp.repeat(d_ssd[i], HEADDIM)[None, :], g_ssd_norm[i][None, :], s5_d[i][None, :],
                  s5_w_glu[i].astype(bf16), s5_b_glu[i][None, :], fnet_w[i].astype(bf16), fnet_b[i][None, :],
                  w_out[i].astype(bf16), g_post[i][None, :])
    return xs[:, :SEQ, :]
```

```python
import functools

import numpy as np
import jax
import jax.numpy as jnp
from jax import lax
from jax.experimental import pallas as pl
from jax.experimental.pallas import tpu as pltpu

f32 = jnp.float32
bf16 = jnp.bfloat16

D_MODEL = 1024
SEQ = 4096
CTX = 256
NROW = SEQ + CTX
DEPTH = 4
GRID_W = 64
D_SSD = 1024
HEADDIM = 64
HEADS = 16
GROUPS = 2
HPG = HEADS // GROUPS
NSTATE = 128
CHUNK = 128
NCHUNK = NROW // CHUNK
NCHUNK_LAT = SEQ // CHUNK
D_XBC = D_SSD + 2 * GROUPS * NSTATE
D_S5 = 512
S5_G = 32
S5_C = 16
S5_P = 64
S5_T = 8
S5_NCH = NROW // S5_T
S5_NCH_LAT = SEQ // S5_T
S5_NCH_CTX = CTX // S5_T
S5_LB = 4
S5_Q = 4
S5_ROWS = S5_NCH * 8
D_FNET = 512
FN_HEADS = 4
FN_HD = 128
DT_PAD = 128
D_PACK = D_SSD + D_XBC + 4 * 512 + DT_PAD
EPS = 1e-6
TM = 256
NT = NROW // TM
VMEM_LIMIT = 56 * 1024 * 1024


def _sigmoid(v):
    return 1.0 / (1.0 + jnp.exp(-v))


def _silu(v):
    return v * _sigmoid(v)


def _split2(v):
    hi = v.astype(bf16)
    lo = (v - hi.astype(f32)).astype(bf16)
    return hi, lo


def _split3(v):
    hi = v.astype(bf16)
    r = v - hi.astype(f32)
    mid = r.astype(bf16)
    lo = (r - mid.astype(f32)).astype(bf16)
    return hi, mid, lo


def _dot(a, b):
    return jnp.dot(a, b, preferred_element_type=f32)


def _dot_nt(a, b):
    return lax.dot_general(a, b, (((1,), (1,)), ((), ())), preferred_element_type=f32)


def _mod_kernel(c_ref, w_ref, b_ref, o_ref):
    s = _silu(c_ref[...])
    o_ref[0] = _dot(s.astype(bf16), w_ref[0].astype(bf16)) + b_ref[0]


def _mod_all(c8, w_mod, b_mod):
    depth = w_mod.shape[0]
    return pl.pallas_call(
        _mod_kernel,
        out_shape=jax.ShapeDtypeStruct((depth, 8, 3 * D_MODEL), f32),
        grid=(depth, 3),
        in_specs=[
            pl.BlockSpec((8, D_MODEL), lambda l, n: (0, 0)),
            pl.BlockSpec((1, D_MODEL, D_MODEL), lambda l, n: (l, 0, n)),
            pl.BlockSpec((1, 1, D_MODEL), lambda l, n: (l, 0, n)),
        ],
        out_specs=pl.BlockSpec((1, 8, D_MODEL), lambda l, n: (l, 0, n)),
        name="mod",
    )(c8, w_mod, b_mod)


_PROJ_WIDTHS = (D_SSD, D_XBC, 512, 512, 512, 512, DT_PAD)
_PROJ_DTYPES = (bf16, bf16, f32, bf16, f32, bf16, f32)


def _inproj_kernel(x_ref, mod_ref, g_ref, w_ref, *out_refs):
    b = pl.program_id(0)
    j = pl.program_id(1)
    row = jnp.where(j == NT - 1, pl.num_programs(0), b)
    m = mod_ref[pl.ds(row, 1), :]
    shift = m[:, 0:D_MODEL]
    scale = m[:, D_MODEL:2 * D_MODEL]
    x = x_ref[0]
    ms = jnp.mean(x * x, axis=-1, keepdims=True)
    xn = x * lax.rsqrt(ms + EPS) * g_ref[...]
    h = (xn * (1.0 + scale) + shift).astype(bf16)
    off = 0
    for ref, width in zip(out_refs, _PROJ_WIDTHS):
        ref[0] = _dot(h, w_ref[:, off:off + width]).astype(ref.dtype)
        off += width


def _inproj(x, mod_l, g_pre, w_pack):
    nb = x.shape[0]
    outs = tuple(jax.ShapeDtypeStruct((nb, NROW, w), dt) for w, dt in zip(_PROJ_WIDTHS, _PROJ_DTYPES))
    return pl.pallas_call(
        _inproj_kernel,
        out_shape=outs,
        grid=(nb, NT),
        in_specs=[
            pl.BlockSpec((1, TM, D_MODEL), lambda b, j: (b, j, 0)),
            pl.BlockSpec((8, 3 * D_MODEL), lambda b, j: (0, 0)),
            pl.BlockSpec((1, D_MODEL), lambda b, j: (0, 0)),
            pl.BlockSpec((D_MODEL, D_PACK), lambda b, j: (0, 0)),
        ],
        out_specs=tuple(pl.BlockSpec((1, TM, w), lambda b, j: (b, j, 0)) for w in _PROJ_WIDTHS),
        compiler_params=pltpu.CompilerParams(vmem_limit_bytes=VMEM_LIMIT),
        name="inproj",
    )(x, mod_l, g_pre, w_pack)


def _conv_kernel(x_ref, w_ref, b_ref, o_ref, pad_ref):
    zero = jnp.zeros((GRID_W, 128), f32)
    pad_ref[0:GRID_W, :] = zero
    pad_ref[GRID_W + SEQ:2 * GRID_W + SEQ, :] = zero
    pad_ref[GRID_W:GRID_W + SEQ, :] = x_ref[0, 0:SEQ, :].astype(f32)
    w = w_ref[...]
    bias = b_ref[...]
    rid = lax.broadcasted_iota(jnp.int32, (GRID_W, 128), 0)
    first = rid == 0
    last = rid == GRID_W - 1

    def body(r, carry):
        acc = jnp.zeros((GRID_W, 128), f32) + bias
        for i in range(3):
            start = pl.multiple_of((r + i) * GRID_W, GRID_W)
            v = pad_ref[pl.ds(start, GRID_W), :]
            vm = jnp.where(first, 0.0, pltpu.roll(v, 1, 0))
            vp = jnp.where(last, 0.0, pltpu.roll(v, GRID_W - 1, 0))
            acc = acc + w[3 * i:3 * i + 1] * vm + w[3 * i + 1:3 * i + 2] * v + w[3 * i + 2:3 * i + 3] * vp
        o_ref[0, pl.ds(pl.multiple_of(r * GRID_W, GRID_W), GRID_W), :] = _silu(acc).astype(o_ref.dtype)
        return carry

    lax.fori_loop(0, SEQ // GRID_W, body, 0)

    xc = x_ref[0, SEQ:NROW, :].astype(f32)
    cid =lax.broadcasted_iota(jnp.int32, (CTX, 128), 0)
    vm = jnp.where(cid == 0, 0.0, pltpu.roll(xc, 1, 0))
    vp = jnp.where(cid == CTX - 1, 0.0, pltpu.roll(xc, CTX - 1, 0))
    acc = bias + w[3:4] * vm + w[4:5] * xc + w[5:6] * vp
    o_ref[0, SEQ:NROW, :] = _silu(acc).astype(o_ref.dtype)


def _conv(xbc, w9, cb):
    nb = xbc.shape[0]
    return pl.pallas_call(
        _conv_kernel,
        out_shape=jax.ShapeDtypeStruct((nb, NROW, D_XBC), bf16),
        grid=(nb, D_XBC // 128),
        in_specs=[
            pl.BlockSpec((1, NROW, 128), lambda b, c: (b, 0, c)),
            pl.BlockSpec((9, 128), lambda b, c: (0, c)),
            pl.BlockSpec((1, 128), lambda b, c: (0, c)),
        ],
        out_specs=pl.BlockSpec((1, NROW, 128), lambda b, c: (b, 0, c)),
        scratch_shapes=[pltpu.VMEM((SEQ + 2 * GRID_W, 128), f32)],
        name="conv",
    )(xbc, w9, cb)


def _ssd_kernel(xf_ref, bf_ref, cf_ref, dtf_ref, xb_ref, bb_ref, cb_ref, dtb_ref,
                dtbias_ref, alog_ref, e_ref, eye_ref, dsk_ref, yf_ref, yb_ref, h_ref, ybuf_ref):
    s = pl.program_id(1)

    @pl.when(s == 0)
    def _():
        h_ref[...] = jnp.zeros(h_ref.shape, f32)

    expand = e_ref[...]
    eye = eye_ref[...]
    ii = lax.broadcasted_iota(jnp.int32, (CHUNK, CHUNK), 0)
    jj = lax.broadcasted_iota(jnp.int32, (CHUNK, CHUNK), 1)
    dirs = ((xf_ref, bf_ref, cf_ref, dtf_ref, yf_ref), (xb_ref, bb_ref, cb_ref, dtb_ref, yb_ref))
    for d, (x_ref, b_ref, c_ref, dt_ref, y_ref) in enumerate(dirs):
        keep = (ii >= jj) if d == 0 else (ii <= jj)
        tri = keep.astype(bf16)
        x = x_ref[0].astype(f32)
        bm = b_ref[0].astype(f32)
        cm = c_ref[0]
        v = dt_ref[0][:, HEADS * d:HEADS * (d + 1)] + dtbias_ref[d:d + 1, :]
        dt = jnp.maximum(v, 0.0) + jnp.log1p(jnp.exp(-jnp.abs(v)))
        a = dt * (-jnp.exp(alog_ref[d:d + 1, :]))
        a3 = _split3(a)
        acum = _dot(tri, a3[0]) + _dot(tri, a3[1]) + _dot(tri, a3[2])
        total = acum[CHUNK - 1:CHUNK] if d == 0 else acum[0:1]
        c3 = _split3(acum)
        acum_t = _dot_nt(eye, c3[0]) + _dot_nt(eye, c3[1]) + _dot_nt(eye, c3[2])
        stack = jnp.concatenate(
            [dt, jnp.exp(acum), jnp.exp(total - acum), jnp.broadcast_to(jnp.exp(total), (8, HEADS))], axis=0)
        s_hi, s_lo = _split2(stack)
        ex = _dot(s_hi, expand) + _dot(s_lo, expand)
        dt_e = ex[0:CHUNK]
        ea_e = ex[CHUNK:2 * CHUNK]
        dte_e = ex[2 * CHUNK:3 * CHUNK]
        cd_e = ex[3 * CHUNK:3 * CHUNK + 1]
        xdt = x * dt_e
        xdt_b = xdt.astype(bf16)
        xw_b = (xdt * dte_e).astype(bf16)
        for g in range(GROUPS):
            bg = bm[:, NSTATE * g:NSTATE * (g + 1)]
            cg_b = cm[:, NSTATE * g:NSTATE * (g + 1)].astype(bf16)
            bg_t = bg.T.astype(bf16)
            cbm = _dot(cg_b, bg_t)
            hprev = h_ref[d, g]
            gl = slice(HPG * HEADDIM * g, HPG * HEADDIM * (g + 1))
            yoff = _dot(cg_b, hprev.astype(bf16)) * ea_e[:, gl]
            for hh in range(HPG):
                h = HPG * g + hh
                seg = acum[:, h:h + 1] - acum_t[h:h + 1, :]
                dec = jnp.where(keep, jnp.exp(jnp.where(keep, seg, 0.0)), 0.0)
                gmat = (cbm * dec).astype(bf16)
                yd = _dot(gmat, xdt_b[:, HEADDIM * h:HEADDIM * (h + 1)])
                ybuf_ref[:, HEADDIM * h:HEADDIM * (h + 1)] = yd + yoff[:, HEADDIM * hh:HEADDIM * (hh + 1)]
            st = _dot(bg_t, xw_b[:, gl])
            h_ref[d, g] = hprev * cd_e[:, gl] + st
        y = ybuf_ref[...]
        if d == 0:
            y = y + dsk_ref[...] * x
        y_ref[0] = y.astype(y_ref.dtype)


def _ssd(act, dtp, dt_bias, a_log, expand, eye, dsk):
    nb = act.shape[0]

    def fwd_chunk(s):
        return jnp.where(s < 2, NCHUNK_LAT + s, s - 2)

    def bwd_chunk(s):
        return NCHUNK - 1 - s

    def specs(chunk_fn):
        return [
            pl.BlockSpec((1, CHUNK, D_SSD), lambda b, s: (b, chunk_fn(s), 0)),
            pl.BlockSpec((1, CHUNK, 2 * NSTATE), lambda b, s: (b, chunk_fn(s), D_SSD // (2 * NSTATE))),
            pl.BlockSpec((1, CHUNK, 2 * NSTATE), lambda b, s: (b, chunk_fn(s), D_SSD // (2 * NSTATE) + 1)),
            pl.BlockSpec((1, CHUNK, DT_PAD), lambda b, s: (b, chunk_fn(s), 0)),
        ]

    out = jax.ShapeDtypeStruct((nb, NROW, D_SSD), bf16)
    return pl.pallas_call(
        _ssd_kernel,
        out_shape=(out, out),
        grid=(nb, NCHUNK),
        in_specs=specs(fwd_chunk) + specs(bwd_chunk) + [
            pl.BlockSpec((2, HEADS), lambda b, s: (0, 0)),
            pl.BlockSpec((2, HEADS), lambda b, s: (0, 0)),
            pl.BlockSpec((HEADS, D_SSD), lambda b, s: (0, 0)),
            pl.BlockSpec((HEADS, HEADS), lambda b, s: (0, 0)),
            pl.BlockSpec((1, D_SSD), lambda b, s: (0, 0)),
        ],
        out_specs=(
            pl.BlockSpec((1, CHUNK, D_SSD), lambda b, s: (b, fwd_chunk(s), 0)),
            pl.BlockSpec((1, CHUNK, D_SSD), lambda b, s: (b, bwd_chunk(s), 0)),
        ),
        scratch_shapes=[pltpu.VMEM((2, GROUPS, NSTATE, HPG * HEADDIM), f32), pltpu.VMEM((CHUNK, D_SSD), f32)],
        compiler_params=pltpu.CompilerParams(dimension_semantics=("arbitrary", "arbitrary")),
        name="ssd",
    )(act, act, act, dtp, act, act, act, dtp, dt_bias, a_log, expand, eye, dsk)


def _s5_matrices(lam_re, lam_im, log_step, b_re, b_im, c_re, c_im):
    hp = lax.Precision.HIGHEST
    t = S5_T
    lr = lam_re.astype(f32)
    li = lam_im.astype(f32)
    step = jnp.exp(log_step.astype(f32))[..., None]
    mag = jnp.exp(lr * step)
    ar = mag * jnp.cos(li * step)
    ai = mag * jnp.sin(li * step)
    den = lr * lr + li * li
    fr = ((ar - 1.0) * lr + ai * li) / den
    fi = (ai * lr - (ar - 1.0) * li) / den
    br = b_re.astype(f32)
    bi = b_im.astype(f32)
    bbr = fr[..., None] * br - fi[..., None] * bi
    bbi = fr[..., None] * bi + fi[..., None] * br
    taus = jnp.arange(t + 1, dtype=f32)[:, None]
    pmag = jnp.exp(taus * (lr * step)[:, :, None, :])
    pang = taus * (li * step)[:, :, None, :]
    pwr = pmag * jnp.cos(pang)
    pwi = pmag * jnp.sin(pang)
    cr = c_re.astype(f32)[:, :, None]
    ci = c_im.astype(f32)[:, :, None]
    dr = cr * pwr[:, :, :, None, :] - ci * pwi[:, :, :, None, :]
    di = cr * pwi[:, :, :, None, :] + ci * pwr[:, :, :, None, :]
    kern = (jnp.einsum('dgtkp,dgpj->dgtkj', dr, bbr, precision=hp)
            - jnp.einsum('dgtkp,dgpj->dgtkj', di, bbi, precision=hp))
    tt = jnp.arange(t)[:, None]
    ss = jnp.arange(t)[None, :]
    kf = kern[0][:, jnp.clip(tt - ss, 0, t)]
    kb = kern[1][:, jnp.clip(ss - tt, 0, t)]
    m = (jnp.where((tt >= ss)[None, :, :, None, None], kf, 0.0)
         + jnp.where((ss >= tt)[None, :, :, None, None], kb, 0.0))
    m_t = m.transpose(0, 2, 4, 1, 3).reshape(S5_G, t * S5_C, t * S5_C)
    ef = jnp.arange(t - 1, -1, -1)
    eb = jnp.arange(t)

    def state_map(d, exps):
        pr = pwr[d][:, exps]
        pi = pwi[d][:, exps]
        re = pr[:, :, None, :] * bbr[d].transpose(0, 2, 1)[:, None] - pi[:, :, None, :] * bbi[d].transpose(0, 2, 1)[:, None]
        im = pr[:, :, None, :] * bbi[d].transpose(0, 2, 1)[:, None] + pi[:, :, None, :] * bbr[d].transpose(0, 2, 1)[:, None]
        return re.reshape(S5_G, t * S5_C, S5_P), im.reshape(S5_G, t * S5_C, S5_P)

    pfr, pfi = state_map(0, ef)
    pbr, pbi = state_map(1, eb)

    def read_map(d, exps):
        qr = dr[d][:, exps].transpose(0, 3, 1, 2).reshape(S5_G, S5_P, t * S5_C)
        qi = -di[d][:, exps].transpose(0, 3, 1, 2).reshape(S5_G, S5_P, t * S5_C)
        return qr, qi

    qfr, qfi = read_map(0, jnp.arange(1, t + 1))
    qbr, qbi = read_map(1, jnp.arange(t, 0, -1))

    gl = S5_G // S5_LB
    g2 = gl // S5_Q
    eye_g = jnp.eye(gl, dtype=f32)
    eye_q = jnp.eye(S5_Q, dtype=f32)
    eye_2 = jnp.eye(g2, dtype=f32)
    lanes = t * gl * S5_C
    wbig = jnp.einsum('lgsjtk,gh->lsgjthk', m_t.reshape(S5_LB, gl, t, S5_C, t, S5_C), eye_g)
    wbig = wbig.reshape(S5_LB, lanes, lanes)
    pc = jnp.stack([pfr, pfi, pbr, pbi]).reshape(4, S5_LB, S5_Q, g2, t, S5_C, S5_P)
    pbig = jnp.einsum('alqgsjp,qr,gh->lqsrhjagp', pc, eye_q, eye_2).reshape(S5_LB, S5_Q, lanes, 4 * g2 * S5_P)
    qc = jnp.stack([qfr, qfi, qbr, qbi]).reshape(4, S5_LB, S5_Q, g2, S5_P, t, S5_C)
    qbig = jnp.einsum('alqgptk,qr,gh->lqagptrhk', qc, eye_q, eye_2).reshape(S5_LB, S5_Q, 4 * g2 * S5_P, lanes)
    npiece = S5_LB * S5_Q
    tr = jnp.repeat(pwr[:, :, t, :].reshape(2, npiece, 1, 128), 4, axis=0).transpose(1, 0, 2, 3).reshape(npiece, 8, 128)
    ti = jnp.repeat(pwi[:, :, t, :].reshape(2, npiece, 1, 128), 4, axis=0).transpose(1, 0, 2, 3).reshape(npiece, 8, 128)
    return wbig.astype(bf16), pbig.astype(bf16), qbig.astype(bf16), tr, ti


def _s5_xcat(u_ref):
    return jnp.concatenate([u_ref[0, pl.ds(t, S5_NCH, stride=S5_T), :] for t in range(S5_T)], axis=-1).astype(bf16)


def _s5a_kernel(u_ref, p_ref, sr_ref, si_ref):
    b = pl.program_id(2)
    r = _dot(_s5_xcat(u_ref), p_ref[0, 0])
    for d in range(2):
        k = 4 * d + b
        sr_ref[0, pl.ds(k, S5_NCH, stride=8), :] = r[:, 256 * d:256 * d + 128]
        si_ref[0, pl.ds(k, S5_NCH, stride=8), :] = r[:, 256 * d + 128:256 * d + 256]


def _s5a(u, pbig):
    nb = u.shape[0]
    out = jax.ShapeDtypeStruct((S5_LB * S5_Q, S5_ROWS, 128), f32)
    ospec = pl.BlockSpec((1, S5_ROWS, 128), lambda l, q, b: (l * S5_Q + q, 0, 0))
    return pl.pallas_call(
        _s5a_kernel,
        out_shape=(out, out),
        grid=(S5_LB, S5_Q, nb),
        in_specs=[
            pl.BlockSpec((1, NROW, 128), lambda l, q, b: (b, 0, l)),
            pl.BlockSpec((1, 1, S5_T * 128, 512), lambda l, q, b: (l, q, 0, 0)),
        ],
        out_specs=(ospec, ospec),
        compiler_params=pltpu.CompilerParams(dimension_semantics=("arbitrary", "arbitrary", "arbitrary")),
        name="s5_state",
    )(u, pbig)


def _s5scan_kernel(sr_ref, si_ref, tr_ref, ti_ref, hr_ref, hi_ref):
    npc = sr_ref.shape[0]
    is_fwd = lax.broadcasted_iota(jnp.int32, (8, 128), 0) < 4
    is_bwd = jnp.logical_not(is_fwd)

    def body(s, carry):
        cf = jnp.where(s < S5_NCH_CTX, S5_NCH_LAT + s, s - S5_NCH_CTX)
        cb = S5_NCH - 1 - s
        rf = pl.multiple_of(cf * 8, 8)
        rb = pl.multiple_of(cb * 8, 8)
        new = []
        for i in range(npc):
            hr = carry[2 * i]
            hi = carry[2 * i + 1]
            xr = jnp.where(is_fwd, sr_ref[i, pl.ds(rf, 8), :], sr_ref[i, pl.ds(rb, 8), :])
            xi = jnp.where(is_fwd, si_ref[i, pl.ds(rf, 8), :], si_ref[i, pl.ds(rb, 8), :])
            pltpu.store(hr_ref.at[i, pl.ds(rf, 8), :], hr, mask=is_fwd)
            pltpu.store(hr_ref.at[i, pl.ds(rb, 8), :], hr, mask=is_bwd)
            pltpu.store(hi_ref.at[i, pl.ds(rf, 8), :], hi, mask=is_fwd)
            pltpu.store(hi_ref.at[i, pl.ds(rb, 8), :], hi, mask=is_bwd)
            tr = tr_ref[i]
            ti = ti_ref[i]
            new.append(tr * hr - ti * hi + xr)
            new.append(tr * hi + ti * hr + xi)
        return tuple(new)

    z = jnp.zeros((8, 128), f32)
    lax.fori_loop(0, S5_NCH, body, (z,) * (2 * npc))


def _s5scan(sr, si, tr, ti):
    npc = 2
    big = pl.BlockSpec((npc, S5_ROWS, 128), lambda i: (i, 0, 0))
    small = pl.BlockSpec((npc, 8, 128), lambda i: (i, 0, 0))
    out = jax.ShapeDtypeStruct(sr.shape, f32)
    return pl.pallas_call(
        _s5scan_kernel,
        out_shape=(out, out),
        grid=(sr.shape[0] // npc,),
        in_specs=[big, big, small, small],
        out_specs=(big, big),
        compiler_params=pltpu.CompilerParams(vmem_limit_bytes=VMEM_LIMIT),
        name="s5_scan",
    )(sr, si, tr, ti)


def _s5c_kernel(u_ref, hr_ref, hi_ref, w_ref, q_ref, y_ref, acc_ref):
    b = pl.program_id(1)
    q = pl.program_id(2)
    parts = []
    for d in range(2):
        k = 4 * d + b
        parts.append(hr_ref[0, pl.ds(k, S5_NCH, stride=8), :])
        parts.append(hi_ref[0, pl.ds(k, S5_NCH, stride=8), :])
    hin = jnp.concatenate(parts, axis=-1).astype(bf16)
    contrib = _dot(hin, q_ref[0, 0])

    @pl.when(q == 0)
    def _():
        acc_ref[...] = contrib + _dot(_s5_xcat(u_ref), w_ref[0])

    @pl.when(q > 0)
    def _():
        acc_ref[...] += contrib

    @pl.when(q == S5_Q - 1)
    def _():
        for t in range(S5_T):
            y_ref[0, pl.ds(t, S5_NCH, stride=S5_T), :] = acc_ref[:, 128 * t:128 * (t + 1)]


def _s5c(u, hr, hi, wbig, qbig):
    nb = u.shape[0]
    hspec = pl.BlockSpec((1, S5_ROWS, 128), lambda l, b, q: (l * S5_Q + q, 0, 0))
    return pl.pallas_call(
        _s5c_kernel,
        out_shape=jax.ShapeDtypeStruct((nb, NROW, D_S5), f32),
        grid=(S5_LB, nb, S5_Q),
        in_specs=[
            pl.BlockSpec((1, NROW, 128), lambda l, b, q: (b, 0, l)),
            hspec, hspec,
            pl.BlockSpec((1, S5_T * 128, S5_T * 128), lambda l, b, q: (l, 0, 0)),
            pl.BlockSpec((1, 1, 512, S5_T * 128), lambda l, b, q: (l, q, 0, 0)),
        ],
        out_specs=pl.BlockSpec((1, NROW, 128), lambda l, b, q: (b, 0, l)),
        scratch_shapes=[pltpu.VMEM((S5_NCH, S5_T * 128), f32)],
        compiler_params=pltpu.CompilerParams(dimension_semantics=("arbitrary", "arbitrary", "arbitrary")),
        name="s5_read",
    )(u, hr, hi, wbig, qbig)


def _s5_mix(u, mats):
    wbig, pbig, qbig, tr, ti = mats
    sr, si = _s5a(u, pbig)
    hr, hi = _s5scan(sr, si, tr, ti)
    return _s5c(u, hr, hi, wbig, qbig)


FN_N1 = 8
FN_N2 = SEQ // FN_N1
FN_TILE = 16


def _fnet_consts():
    k = np.arange(FN_N2)
    ang = 2.0 * np.pi * np.outer(k, k) / FN_N2
    g = np.concatenate([np.cos(ang), -np.sin(ang)], axis=0) / np.sqrt(SEQ)
    r = np.arange(FN_N1)[:, None, None]
    kc = np.arange(FN_N2)[None, :, None]
    tw = 2.0 * np.pi * r * kc / SEQ * np.ones((1, 1, FN_HD))
    twc = np.cos(tw).reshape(SEQ, FN_HD)
    tws = np.sin(tw).reshape(SEQ, FN_HD)
    j = np.arange(FN_HD)
    angc = 2.0 * np.pi * np.outer(j, j) / FN_HD
    ch = np.concatenate([np.cos(angc), np.sin(angc)], axis=0) / np.sqrt(FN_HD)
    t = np.arange(CTX)
    angt = 2.0 * np.pi * np.outer(t, t) / CTX
    fc = np.concatenate([np.cos(angt), -np.sin(angt)], axis=0) / np.sqrt(CTX)
    as32 = lambda a: jnp.asarray(a.astype(np.float32))
    return as32(g), as32(twc), as32(tws), as32(ch), as32(fc)


def _dft8(a):
    h = 0.7071067811865476
    add = lambda p, q: (p[0] + q[0], p[1] + q[1])
    sub = lambda p, q: (p[0] - q[0], p[1] - q[1])
    mji = lambda p: (p[1], -p[0])
    w1 = lambda p: ((p[0] + p[1]) * h, (p[1] - p[0]) * h)
    w3 = lambda p: ((p[1] - p[0]) * h, -(p[0] + p[1]) * h)
    b0, b1 = add(a[0], a[4]), sub(a[0], a[4])
    b2, b3 = add(a[2], a[6]), sub(a[2], a[6])
    b4, b5 = add(a[1], a[5]), sub(a[1], a[5])
    b6, b7 = add(a[3], a[7]), sub(a[3], a[7])
    c0, c2 = add(b0, b2), sub(b0, b2)
    c1, c3 = add(b1, mji(b3)), sub(b1, mji(b3))
    d0, d2 = add(b4, b6), sub(b4, b6)
    d1, d3 = add(b5, mji(b7)), sub(b5, mji(b7))
    t1, t2, t3 = w1(d1), mji(d2), w3(d3)
    return [add(c0, d0), add(c1, t1), add(c2, t2), add(c3, t3),
            sub(c0, d0), sub(c1, t1), sub(c2, t2), sub(c3, t3)]


def _fnet_kernel(xa_ref, xb_ref, g_ref, twc_ref, tws_ref, ch_ref, fc_ref, o_ref, yr_ref, yi_ref):
    g = g_ref[...]
    chm = ch_ref[...]
    x_refs = (xa_ref, xb_ref)
    for r in range(FN_N1):
        xr = jnp.concatenate([x[0, pl.ds(r, FN_N2, stride=FN_N1), :] for x in x_refs], axis=-1).astype(bf16)
        y = _dot(g, xr)
        cs = twc_ref[FN_N2 * r:FN_N2 * (r + 1), :]
        sn = tws_ref[FN_N2 * r:FN_N2 * (r + 1), :]
        cs = jnp.concatenate([cs, cs], axis=-1)
        sn = jnp.concatenate([sn, sn], axis=-1)
        yr = y[0:FN_N2]
        yi = y[FN_N2:2 * FN_N2]
        yr_ref[r] = yr * cs + yi * sn
        yi_ref[r] = yi * cs - yr * sn

    def butterfly(i, carry):
        rows = pl.ds(pl.multiple_of(i * FN_TILE, FN_TILE), FN_TILE)
        for half in range(2):
            cols = slice(FN_HD * half, FN_HD * (half + 1))
            out = _dft8([(yr_ref[r, rows, cols], yi_ref[r, rows, cols]) for r in range(FN_N1)])
            for kr in range(FN_N1):
                yr_ref[kr, rows, cols] = out[kr][0]
                yi_ref[kr, rows, cols] = out[kr][1]
        return carry

    lax.fori_loop(0, FN_N2 // FN_TILE, butterfly, 0)

    for kr in range(FN_N1):
        for hd in range(2):
            cols = slice(FN_HD * hd, FN_HD * (hd + 1))
            v = jnp.concatenate([yr_ref[kr, :, cols], yi_ref[kr, :, cols]], axis=-1).astype(bf16)
            o_ref[0, FN_N2 * kr:FN_N2 * (kr + 1), cols] = _dot(v, chm).astype(o_ref.dtype)

    for hd in range(2):
        cols = slice(FN_HD * hd, FN_HD * (hd + 1))
        xc = x_refs[hd][0, SEQ:NROW, :].astype(bf16)
        vc = _dot(fc_ref[...], xc)
        v = jnp.concatenate([vc[0:CTX], vc[CTX:2 * CTX]], axis=-1).astype(bf16)
        o_ref[0, SEQ:NROW, cols] = _dot(v, chm).astype(o_ref.dtype)


def _fnet(fu, consts):
    nb = fu.shape[0]
    g, twc, tws, ch, fc = consts
    full = lambda a: pl.BlockSpec(a.shape, lambda b, h: (0, 0))
    return pl.pallas_call(
        _fnet_kernel,
        out_shape=jax.ShapeDtypeStruct((nb, NROW, D_FNET), bf16),
        grid=(nb, FN_HEADS // 2),
        in_specs=[pl.BlockSpec((1, NROW, FN_HD), lambda b, h: (b, 0, 2 * h)),
                  pl.BlockSpec((1, NROW, FN_HD), lambda b, h: (b, 0, 2 * h + 1)),
                  full(g), full(twc), full(tws), full(ch), full(fc)],
        out_specs=pl.BlockSpec((1, NROW, 2 * FN_HD), lambda b, h: (b, 0, h)),
        scratch_shapes=[pltpu.VMEM((FN_N1, FN_N2, 2 * FN_HD), f32), pltpu.VMEM((FN_N1, FN_N2, 2 * FN_HD), f32)],
        compiler_params=pltpu.CompilerParams(vmem_limit_bytes=VMEM_LIMIT),
        name="fnet",
    )(fu, fu, g, twc, tws, ch, fc)


def _out_kernel(x_ref, yf_ref, yb_ref, z_ref, ys_ref, u_ref, gs_ref, sp_ref, fg_ref,
                mod_ref, gn_ref, sd_ref, wg_ref, bg_ref, fw_ref, fb_ref, wo_ref, gp_ref, o_ref):
    b = pl.program_id(0)
    j = pl.program_id(1)
    row = jnp.where(j == NT - 1, pl.num_programs(0), b)
    gate = mod_ref[pl.ds(row, 1), 2 * D_MODEL:3 * D_MODEL]
    y = yf_ref[0].astype(f32) + yb_ref[0].astype(f32)
    v = y * _silu(z_ref[0].astype(f32))
    gw = D_SSD // GROUPS
    acc = None
    for g in range(GROUPS):
        vg = v[:, gw * g:gw * (g + 1)]
        vg = vg * lax.rsqrt(jnp.mean(vg * vg, axis=-1, keepdims=True) + EPS) * gn_ref[:, gw * g:gw * (g + 1)]
        part = _dot(vg.astype(bf16), wo_ref[gw * g:gw * (g + 1), :])
        acc = part if acc is None else acc + part
    ys = ys_ref[0] + sd_ref[...] * u_ref[0]
    gl = 0.5 * ys * (1.0 + jnp.tanh(0.7978845608028654 * (ys + 0.044715 * ys * ys * ys)))
    s5o = gl * _sigmoid(_dot(gl.astype(bf16), wg_ref[...]) + bg_ref[...]) * _silu(gs_ref[0].astype(f32))
    acc = acc + _dot(s5o.astype(bf16), wo_ref[D_SSD:D_SSD + D_S5, :])
    fo = (_dot(sp_ref[0].astype(bf16), fw_ref[...]) + fb_ref[...]) * _silu(fg_ref[0].astype(f32))
    acc = acc + _dot(fo.astype(bf16), wo_ref[D_SSD + D_S5:D_SSD + D_S5 + D_FNET, :])
    yn = acc * lax.rsqrt(jnp.mean(acc * acc, axis=-1, keepdims=True) + EPS) * gp_ref[...]
    o_ref[0] = x_ref[0] + gate * yn


def _out(x, yf, yb, z, ys, u, gs, sp, fg, mod_l, gn, sd, wg, bg, fw, fb, wo, gp):
    nb = x.shape[0]
    row = lambda w: pl.BlockSpec((1, TM, w), lambda b, j: (b, j, 0))
    full = lambda a: pl.BlockSpec(a.shape, lambda b, j: (0, 0))
    return pl.pallas_call(
        _out_kernel,
        out_shape=jax.ShapeDtypeStruct((nb, NROW, D_MODEL), f32),
        grid=(nb, NT),
        in_specs=[row(D_MODEL), row(D_SSD), row(D_SSD), row(D_SSD), row(512), row(512), row(512),
                  row(512), row(512), full(mod_l), full(gn), full(sd), full(wg), full(bg),
                  full(fw), full(fb), full(wo), full(gp)],
        out_specs=row(D_MODEL),
        compiler_params=pltpu.CompilerParams(vmem_limit_bytes=VMEM_LIMIT),
        name="out",
    )(x, yf, yb, z, ys, u, gs, sp, fg, mod_l, gn, sd, wg, bg, fw, fb, wo, gp)


def _pack_w_in(w):
    z, xbc, dt, su, sg, fu, fg = jnp.split(w, (1024, 2560, 2592, 3104, 3616, 4128), axis=-1)
    dt = jnp.pad(dt, ((0, 0), (0, DT_PAD - 2 * HEADS)))
    return jnp.concatenate([z, xbc, su, sg, fu, fg, dt], axis=-1).astype(bf16)


def kernel(x, c, ctx, c_ctx, w_mod, b_mod, g_pre, g_post, w_in, conv_w, conv_b, dt_bias, a_log, d_ssd, g_ssd_norm, s5_lambda_re, s5_lambda_im, s5_log_step, s5_b_re, s5_b_im, s5_c_re, s5_c_im, s5_d, s5_w_glu, s5_b_glu, fnet_w, fnet_b, w_out):
    nb = x.shape[0]
    depth = w_in.shape[0]
    xs = jnp.concatenate([x, ctx], axis=1)
    c8 = jnp.concatenate([c, c_ctx[None, :], jnp.zeros((8 - nb - 1, D_MODEL), f32)], axis=0)
    mod = _mod_all(c8, w_mod, b_mod[:, None, :])
    expand = jnp.repeat(jnp.eye(HEADS, dtype=f32), HEADDIM, axis=1).astype(bf16)
    eye = jnp.eye(HEADS, dtype=f32).astype(bf16)
    fconsts = tuple(a.astype(bf16) if i in (0, 3, 4) else a for i, a in enumerate(_fnet_consts()))
    for i in range(depth):
        z, xbc, u, gs, fu, fg, dtp = _inproj(xs, mod[i], g_pre[i][None, :], _pack_w_in(w_in[i]))
        act = _conv(xbc, conv_w[i].reshape(9, D_XBC), conv_b[i][None, :])
        yf, yb = _ssd(act, dtp, dt_bias[i], a_log[i], expand, eye, jnp.repeat(d_ssd[i], HEADDIM)[None, :])
        mats = _s5_matrices(s5_lambda_re[i], s5_lambda_im[i], s5_log_step[i], s5_b_re[i], s5_b_im[i],
                            s5_c_re[i], s5_c_im[i])
        ys = _s5_mix(u, mats)
        sp = _fnet(fu, fconsts)
        xs = _out(xs, yf, yb, z, ys, u, gs, sp, fg, mod[i], g_ssd_norm[i][None, :], s5_d[i][None, :],
                  s5_w_glu[i].astype(bf16), s5_b_glu[i][None, :], fnet_w[i].astype(bf16), fnet_b[i][None, :],
                  w_out[i].astype(bf16), g_post[i][None, :])
    return xs[:, :SEQ, :]
```

```python
import functools

import numpy as np
import jax
import jax.numpy as jnp
from jax import lax
from jax.experimental import pallas as pl
from jax.experimental.pallas import tpu as pltpu

f32 = jnp.float32
bf16 = jnp.bfloat16

D_MODEL = 1024
SEQ = 4096
CTX = 256
NROW = SEQ + CTX
DEPTH = 4
GRID_W = 64
D_SSD = 1024
HEADDIM = 64
HEADS = 16
GROUPS = 2
HPG = HEADS // GROUPS
NSTATE = 128
CHUNK = 128
NCHUNK = NROW // CHUNK
NCHUNK_LAT = SEQ // CHUNK
D_XBC = D_SSD + 2 * GROUPS * NSTATE
D_S5 = 512
S5_G = 32
S5_C = 16
S5_P = 64
S5_T = 8
S5_NCH = NROW // S5_T
S5_NCH_LAT = SEQ // S5_T
S5_NCH_CTX = CTX // S5_T
S5_LB = 4
S5_Q = 4
S5_ROWS = S5_NCH * 8
D_FNET = 512
FN_HEADS = 4
FN_HD = 128
DT_PAD = 128
D_PACK = D_SSD + D_XBC + 4 * 512 + DT_PAD
EPS = 1e-6
TM = 256
NT = NROW // TM
VMEM_LIMIT = 56 * 1024 * 1024


def _sigmoid(v):
    return 1.0 / (1.0 + jnp.exp(-v))


def _silu(v):
    return v * _sigmoid(v)


def _split2(v):
    hi = v.astype(bf16)
    lo = (v - hi.astype(f32)).astype(bf16)
    return hi, lo


def _split3(v):
    hi = v.astype(bf16)
    r = v - hi.astype(f32)
    mid = r.astype(bf16)
    lo = (r - mid.astype(f32)).astype(bf16)
    return hi, mid, lo


def _dot(a, b):
    return jnp.dot(a, b, preferred_element_type=f32)


def _dot_nt(a, b):
    return lax.dot_general(a, b, (((1,), (1,)), ((), ())), preferred_element_type=f32)


def _mod_kernel(c_ref, w_ref, b_ref, o_ref):
    s = _silu(c_ref[...])
    o_ref[0] = _dot(s.astype(bf16), w_ref[0].astype(bf16)) + b_ref[0]


def _mod_all(c8, w_mod, b_mod):
    depth = w_mod.shape[0]
    return pl.pallas_call(
        _mod_kernel,
        out_shape=jax.ShapeDtypeStruct((depth, 8, 3 * D_MODEL), f32),
        grid=(depth, 3),
        in_specs=[
            pl.BlockSpec((8, D_MODEL), lambda l, n: (0, 0)),
            pl.BlockSpec((1, D_MODEL, D_MODEL), lambda l, n: (l, 0, n)),
            pl.BlockSpec((1, 1, D_MODEL), lambda l, n: (l, 0, n)),
        ],
        out_specs=pl.BlockSpec((1, 8, D_MODEL), lambda l, n: (l, 0, n)),
        name="mod",
    )(c8, w_mod, b_mod)


_PROJ_WIDTHS = (D_SSD, D_XBC, 512, 512, 512, 512, DT_PAD)
_PROJ_DTYPES = (bf16, bf16, f32, bf16, f32, bf16, f32)


def _inproj_kernel(x_ref, mod_ref, g_ref, w_ref, *out_refs):
    b = pl.program_id(0)
    j = pl.program_id(1)
    row = jnp.where(j == NT - 1, pl.num_programs(0), b)
    m = mod_ref[pl.ds(row, 1), :]
    shift = m[:, 0:D_MODEL]
    scale = m[:, D_MODEL:2 * D_MODEL]
    x = x_ref[0]
    ms = jnp.mean(x * x, axis=-1, keepdims=True)
    xn = x * lax.rsqrt(ms + EPS) * g_ref[...]
    h = (xn * (1.0 + scale) + shift).astype(bf16)
    off = 0
    for ref, width in zip(out_refs, _PROJ_WIDTHS):
        ref[0] = _dot(h, w_ref[:, off:off + width]).astype(ref.dtype)
        off += width


def _inproj(x, mod_l, g_pre, w_pack):
    nb = x.shape[0]
    outs = tuple(jax.ShapeDtypeStruct((nb, NROW, w), dt) for w, dt in zip(_PROJ_WIDTHS, _PROJ_DTYPES))
    return pl.pallas_call(
        _inproj_kernel,
        out_shape=outs,
        grid=(nb, NT),
        in_specs=[
            pl.BlockSpec((1, TM, D_MODEL), lambda b, j: (b, j, 0)),
            pl.BlockSpec((8, 3 * D_MODEL), lambda b, j: (0, 0)),
            pl.BlockSpec((1, D_MODEL), lambda b, j: (0, 0)),
            pl.BlockSpec((D_MODEL, D_PACK), lambda b, j: (0, 0)),
        ],
        out_specs=tuple(pl.BlockSpec((1, TM, w), lambda b, j: (b, j, 0)) for w in _PROJ_WIDTHS),
        compiler_params=pltpu.CompilerParams(vmem_limit_bytes=VMEM_LIMIT),
        name="inproj",
    )(x, mod_l, g_pre, w_pack)


def _conv_kernel(x_ref, w_ref, b_ref, o_ref, pad_ref):
    zero = jnp.zeros((GRID_W, 128), f32)
    pad_ref[0:GRID_W, :] = zero
    pad_ref[GRID_W + SEQ:2 * GRID_W + SEQ, :] = zero
    pad_ref[GRID_W:GRID_W + SEQ, :] = x_ref[0, 0:SEQ, :].astype(f32)
    w = w_ref[...]
    bias = b_ref[...]
    rid = lax.broadcasted_iota(jnp.int32, (GRID_W, 128), 0)
    first = rid == 0
    last = rid == GRID_W - 1

    def body(r, carry):
        acc = jnp.zeros((GRID_W, 128), f32) + bias
        for i in range(3):
            start = pl.multiple_of((r + i) * GRID_W, GRID_W)
            v = pad_ref[pl.ds(start, GRID_W), :]
            vm = jnp.where(first, 0.0, pltpu.roll(v, 1, 0))
            vp = jnp.where(last, 0.0, pltpu.roll(v, GRID_W - 1, 0))
            acc = acc + w[3 * i:3 * i + 1] * vm + w[3 * i + 1:3 * i + 2] * v + w[3 * i + 2:3 * i + 3] * vp
        o_ref[0, pl.ds(pl.multiple_of(r * GRID_W, GRID_W), GRID_W), :] = _silu(acc).astype(o_ref.dtype)
        return carry

    lax.fori_loop(0, SEQ // GRID_W, body, 0)

    xc = x_ref[0, SEQ:NROW, :].astype(f32)
    cid =lax.broadcasted_iota(jnp.int32, (CTX, 128), 0)
    vm = jnp.where(cid == 0, 0.0, pltpu.roll(xc, 1, 0))
    vp = jnp.where(cid == CTX - 1, 0.0, pltpu.roll(xc, CTX - 1, 0))
    acc = bias + w[3:4] * vm + w[4:5] * xc + w[5:6] * vp
    o_ref[0, SEQ:NROW, :] = _silu(acc).astype(o_ref.dtype)


def _conv(xbc, w9, cb):
    nb = xbc.shape[0]
    return pl.pallas_call(
        _conv_kernel,
        out_shape=jax.ShapeDtypeStruct((nb, NROW, D_XBC), bf16),
        grid=(nb, D_XBC // 128),
        in_specs=[
            pl.BlockSpec((1, NROW, 128), lambda b, c: (b, 0, c)),
            pl.BlockSpec((9, 128), lambda b, c: (0, c)),
            pl.BlockSpec((1, 128), lambda b, c: (0, c)),
        ],
        out_specs=pl.BlockSpec((1, NROW, 128), lambda b, c: (b, 0, c)),
        scratch_shapes=[pltpu.VMEM((SEQ + 2 * GRID_W, 128), f32)],
        name="conv",
    )(xbc, w9, cb)


def _ssd_kernel(xf_ref, bf_ref, cf_ref, dtf_ref, xb_ref, bb_ref, cb_ref, dtb_ref,
                dtbias_ref, alog_ref, e_ref, eye_ref, dsk_ref, yf_ref, yb_ref, h_ref, ybuf_ref):
    s = pl.program_id(1)

    @pl.when(s == 0)
    def _():
        h_ref[...] = jnp.zeros(h_ref.shape, f32)

    expand = e_ref[...]
    eye = eye_ref[...]
    ii = lax.broadcasted_iota(jnp.int32, (CHUNK, CHUNK), 0)
    jj = lax.broadcasted_iota(jnp.int32, (CHUNK, CHUNK), 1)
    dirs = ((xf_ref, bf_ref, cf_ref, dtf_ref, yf_ref), (xb_ref, bb_ref, cb_ref, dtb_ref, yb_ref))
    for d, (x_ref, b_ref, c_ref, dt_ref, y_ref) in enumerate(dirs):
        keep = (ii >= jj) if d == 0 else (ii <= jj)
        tri = keep.astype(bf16)
        x = x_ref[0].astype(f32)
        bm = b_ref[0].astype(f32)
        cm = c_ref[0]
        v = dt_ref[0][:, HEADS * d:HEADS * (d + 1)] + dtbias_ref[d:d + 1, :]
        dt = jnp.maximum(v, 0.0) + jnp.log1p(jnp.exp(-jnp.abs(v)))
        a = dt * (-jnp.exp(alog_ref[d:d + 1, :]))
        a3 = _split3(a)
        acum = _dot(tri, a3[0]) + _dot(tri, a3[1]) + _dot(tri, a3[2])
        total = acum[CHUNK - 1:CHUNK] if d == 0 else acum[0:1]
        c3 = _split3(acum)
        acum_t = _dot_nt(eye, c3[0]) + _dot_nt(eye, c3[1]) + _dot_nt(eye, c3[2])
        stack = jnp.concatenate(
            [dt, jnp.exp(acum), jnp.exp(total - acum), jnp.broadcast_to(jnp.exp(total), (8, HEADS))], axis=0)
        s_hi, s_lo = _split2(stack)
        ex = _dot(s_hi, expand) + _dot(s_lo, expand)
        dt_e = ex[0:CHUNK]
        ea_e = ex[CHUNK:2 * CHUNK]
        dte_e = ex[2 * CHUNK:3 * CHUNK]
        cd_e = ex[3 * CHUNK:3 * CHUNK + 1]
        xdt = x * dt_e
        xdt_b = xdt.astype(bf16)
        xw_b = (xdt * dte_e).astype(bf16)
        for g in range(GROUPS):
            bg = bm[:, NSTATE * g:NSTATE * (g + 1)]
            cg_b = cm[:, NSTATE * g:NSTATE * (g + 1)].astype(bf16)
            bg_t = bg.T.astype(bf16)
            cbm = _dot(cg_b, bg_t)
            hprev = h_ref[d, g]
            gl = slice(HPG * HEADDIM * g, HPG * HEADDIM * (g + 1))
            yoff = _dot(cg_b, hprev.astype(bf16)) * ea_e[:, gl]
            for hh in range(HPG):
                h = HPG * g + hh
                seg = acum[:, h:h + 1] - acum_t[h:h + 1, :]
                dec = jnp.where(keep, jnp.exp(jnp.where(keep, seg, 0.0)), 0.0)
                gmat = (cbm * dec).astype(bf16)
                yd = _dot(gmat, xdt_b[:, HEADDIM * h:HEADDIM * (h + 1)])
                ybuf_ref[:, HEADDIM * h:HEADDIM * (h + 1)] = yd + yoff[:, HEADDIM * hh:HEADDIM * (hh + 1)]
            st = _dot(bg_t, xw_b[:, gl])
            h_ref[d, g] = hprev * cd_e[:, gl] + st
        y = ybuf_ref[...]
        if d == 0:
            y = y + dsk_ref[...] * x
        y_ref[0] = y.astype(y_ref.dtype)


def _ssd(act, dtp, dt_bias, a_log, expand, eye, dsk):
    nb = act.shape[0]

    def fwd_chunk(s):
        return jnp.where(s < 2, NCHUNK_LAT + s, s - 2)

    def bwd_chunk(s):
        return NCHUNK - 1 - s

    def specs(chunk_fn):
        return [
            pl.BlockSpec((1, CHUNK, D_SSD), lambda b, s: (b, chunk_fn(s), 0)),
            pl.BlockSpec((1, CHUNK, 2 * NSTATE), lambda b, s: (b, chunk_fn(s), D_SSD // (2 * NSTATE))),
            pl.BlockSpec((1, CHUNK, 2 * NSTATE), lambda b, s: (b, chunk_fn(s), D_SSD // (2 * NSTATE) + 1)),
            pl.BlockSpec((1, CHUNK, DT_PAD), lambda b, s: (b, chunk_fn(s), 0)),
        ]

    out = jax.ShapeDtypeStruct((nb, NROW, D_SSD), bf16)
    return pl.pallas_call(
        _ssd_kernel,
        out_shape=(out, out),
        grid=(nb, NCHUNK),
        in_specs=specs(fwd_chunk) + specs(bwd_chunk) + [
            pl.BlockSpec((2, HEADS), lambda b, s: (0, 0)),
            pl.BlockSpec((2, HEADS), lambda b, s: (0, 0)),
            pl.BlockSpec((HEADS, D_SSD), lambda b, s: (0, 0)),
            pl.BlockSpec((HEADS, HEADS), lambda b, s: (0, 0)),
            pl.BlockSpec((1, D_SSD), lambda b, s: (0, 0)),
        ],
        out_specs=(
            pl.BlockSpec((1, CHUNK, D_SSD), lambda b, s: (b, fwd_chunk(s), 0)),
            pl.BlockSpec((1, CHUNK, D_SSD), lambda b, s: (b, bwd_chunk(s), 0)),
        ),
        scratch_shapes=[pltpu.VMEM((2, GROUPS, NSTATE, HPG * HEADDIM), f32), pltpu.VMEM((CHUNK, D_SSD), f32)],
        compiler_params=pltpu.CompilerParams(dimension_semantics=("arbitrary", "arbitrary")),
        name="ssd",
    )(act, act, act, dtp, act, act, act, dtp, dt_bias, a_log, expand, eye, dsk)


def _s5_matrices(lam_re, lam_im, log_step, b_re, b_im, c_re, c_im):
    hp = lax.Precision.HIGHEST
    t = S5_T
    lr = lam_re.astype(f32)
    li = lam_im.astype(f32)
    step = jnp.exp(log_step.astype(f32))[..., None]
    mag = jnp.exp(lr * step)
    ar = mag * jnp.cos(li * step)
    ai = mag * jnp.sin(li * step)
    den = lr * lr + li * li
    fr = ((ar - 1.0) * lr + ai * li) / den
    fi = (ai * lr - (ar - 1.0) * li) / den
    br = b_re.astype(f32)
    bi = b_im.astype(f32)
    bbr = fr[..., None] * br - fi[..., None] * bi
    bbi = fr[..., None] * bi + fi[..., None] * br
    taus = jnp.arange(t + 1, dtype=f32)[:, None]
    pmag = jnp.exp(taus * (lr * step)[:, :, None, :])
    pang = taus * (li * step)[:, :, None, :]
    pwr = pmag * jnp.cos(pang)
    pwi = pmag * jnp.sin(pang)
    cr = c_re.astype(f32)[:, :, None]
    ci = c_im.astype(f32)[:, :, None]
    dr = cr * pwr[:, :, :, None, :] - ci * pwi[:, :, :, None, :]
    di = cr * pwi[:, :, :, None, :] + ci * pwr[:, :, :, None, :]
    kern = (jnp.einsum('dgtkp,dgpj->dgtkj', dr, bbr, precision=hp)
            - jnp.einsum('dgtkp,dgpj->dgtkj', di, bbi, precision=hp))
    tt = jnp.arange(t)[:, None]
    ss = jnp.arange(t)[None, :]
    kf = kern[0][:, jnp.clip(tt - ss, 0, t)]
    kb = kern[1][:, jnp.clip(ss - tt, 0, t)]
    m = (jnp.where((tt >= ss)[None, :, :, None, None], kf, 0.0)
         + jnp.where((ss >= tt)[None, :, :, None, None], kb, 0.0))
    m_t = m.transpose(0, 2, 4, 1, 3).reshape(S5_G, t * S5_C, t * S5_C)
    ef = jnp.arange(t - 1, -1, -1)
    eb = jnp.arange(t)

    def state_map(d, exps):
        pr = pwr[d][:, exps]
        pi = pwi[d][:, exps]
        re = pr[:, :, None, :] * bbr[d].transpose(0, 2, 1)[:, None] - pi[:, :, None, :] * bbi[d].transpose(0, 2, 1)[:, None]
        im = pr[:, :, None, :] * bbi[d].transpose(0, 2, 1)[:, None] + pi[:, :, None, :] * bbr[d].transpose(0, 2, 1)[:, None]
        return re.reshape(S5_G, t * S5_C, S5_P), im.reshape(S5_G, t * S5_C, S5_P)

    pfr, pfi = state_map(0, ef)
    pbr, pbi = state_map(1, eb)

    def read_map(d, exps):
        qr = dr[d][:, exps].transpose(0, 3, 1, 2).reshape(S5_G, S5_P, t * S5_C)
        qi = -di[d][:, exps].transpose(0, 3, 1, 2).reshape(S5_G, S5_P, t * S5_C)
        return qr, qi

    qfr, qfi = read_map(0, jnp.arange(1, t + 1))
    qbr, qbi = read_map(1, jnp.arange(t, 0, -1))

    gl = S5_G // S5_LB
    npiece = S5_LB * S5_Q
    mt = m_t.reshape(S5_LB, gl, t * S5_C, t * S5_C)
    pcat = jnp.concatenate([pfr, pfi, pbr, pbi], axis=-1).reshape(npiece, gl // S5_Q, t * S5_C, 4 * S5_P)
    qcat = jnp.concatenate([qfr, qfi, qbr, qbi], axis=1).reshape(npiece, gl // S5_Q, 4 * S5_P, t * S5_C)
    tr = jnp.repeat(pwr[:, :, t, :].reshape(2, npiece, 1, 128), 4, axis=0).transpose(1, 0, 2, 3).reshape(npiece, 8, 128)
    ti = jnp.repeat(pwi[:, :, t, :].reshape(2, npiece, 1, 128), 4, axis=0).transpose(1, 0, 2, 3).reshape(npiece, 8, 128)
    return mt.astype(bf16), pcat.astype(bf16), qcat.astype(bf16), tr, ti


def _s5_place_consts():
    gl = S5_G // S5_LB
    g2 = gl // S5_Q
    cplace = np.zeros((gl, S5_T, S5_C, S5_T, gl, S5_C), np.float32)
    for g in range(gl):
        for t in range(S5_T):
            cplace[g, t, np.arange(S5_C), t, g, np.arange(S5_C)] = 1.0
    cstate = np.zeros((g2, 4, S5_P, 4, g2, S5_P), np.float32)
    for g in range(g2):
        for a in range(4):
            cstate[g, a, np.arange(S5_P), a, g, np.arange(S5_P)] = 1.0
    return (jnp.asarray(cplace.reshape(gl, S5_T * S5_C, S5_T * gl * S5_C)),
            jnp.asarray(cstate.reshape(g2, 4 * S5_P, 4 * g2 * S5_P)))


def _s5_xcat(u_ref):
    return jnp.concatenate([u_ref[0, pl.ds(t, S5_NCH, stride=S5_T), :] for t in range(S5_T)], axis=-1).astype(bf16)


def _s5a_kernel(u_ref, pc_ref, cs_ref, sr_ref, si_ref, p_scr):
    q = pl.program_id(1)
    b = pl.program_id(2)
    rows_g = S5_T * S5_C

    @pl.when(b == 0)
    def _():
        p_scr[...] = jnp.zeros(p_scr.shape, bf16)
        for g2 in range(S5_G // S5_LB // S5_Q):
            tmp = _dot(pc_ref[0, g2], cs_ref[g2]).astype(bf16)
            for s in range(S5_T):
                row = pl.multiple_of(s * 128 + (2 * q + g2) * S5_C, S5_C)
                p_scr[pl.ds(row, S5_C), :] = tmp[s * S5_C:(s + 1) * S5_C, :]

    r = _dot(_s5_xcat(u_ref), p_scr[...])
    for d in range(2):
        k = 4 * d + b
        sr_ref[0, pl.ds(k, S5_NCH, stride=8), :] = r[:, 256 * d:256 * d + 128]
        si_ref[0, pl.ds(k, S5_NCH, stride=8), :] = r[:, 256 * d + 128:256 * d + 256]


def _s5a(u, pcat, cstate):
    nb = u.shape[0]
    out = jax.ShapeDtypeStruct((S5_LB * S5_Q, S5_ROWS, 128), f32)
    ospec = pl.BlockSpec((1, S5_ROWS, 128), lambda l, q, b: (l * S5_Q + q, 0, 0))
    return pl.pallas_call(
        _s5a_kernel,
        out_shape=(out, out),
        grid=(S5_LB, S5_Q, nb),
        in_specs=[
            pl.BlockSpec((1, NROW, 128), lambda l, q, b: (b, 0, l)),
            pl.BlockSpec((1,) + pcat.shape[1:], lambda l, q, b: (l * S5_Q + q, 0, 0, 0)),
            pl.BlockSpec(cstate.shape, lambda l, q, b: (0, 0, 0)),
        ],
        out_specs=(ospec, ospec),
        scratch_shapes=[pltpu.VMEM((S5_T * 128, 512), bf16)],
        compiler_params=pltpu.CompilerParams(dimension_semantics=("arbitrary", "arbitrary", "arbitrary")),
        name="s5_state",
    )(u, pcat, cstate)


def _s5scan_kernel(sr_ref, si_ref, tr_ref, ti_ref, hr_ref, hi_ref):
    npc = sr_ref.shape[0]
    is_fwd = lax.broadcasted_iota(jnp.int32, (8, 128), 0) < 4
    is_bwd = jnp.logical_not(is_fwd)

    def body(s, carry):
        cf = jnp.where(s < S5_NCH_CTX, S5_NCH_LAT + s, s - S5_NCH_CTX)
        cb = S5_NCH - 1 - s
        rf = pl.multiple_of(cf * 8, 8)
        rb = pl.multiple_of(cb * 8, 8)
        new = []
        for i in range(npc):
            hr = carry[2 * i]
            hi = carry[2 * i + 1]
            xr = jnp.where(is_fwd, sr_ref[i, pl.ds(rf, 8), :], sr_ref[i, pl.ds(rb, 8), :])
            xi = jnp.where(is_fwd, si_ref[i, pl.ds(rf, 8), :], si_ref[i, pl.ds(rb, 8), :])
            pltpu.store(hr_ref.at[i, pl.ds(rf, 8), :], hr, mask=is_fwd)
            pltpu.store(hr_ref.at[i, pl.ds(rb, 8), :], hr, mask=is_bwd)
            pltpu.store(hi_ref.at[i, pl.ds(rf, 8), :], hi, mask=is_fwd)
            pltpu.store(hi_ref.at[i, pl.ds(rb, 8), :], hi, mask=is_bwd)
            tr = tr_ref[i]
            ti = ti_ref[i]
            new.append(tr * hr - ti * hi + xr)
            new.append(tr * hi + ti * hr + xi)
        return tuple(new)

    z = jnp.zeros((8, 128), f32)
    lax.fori_loop(0, S5_NCH, body, (z,) * (2 * npc))


def _s5scan(sr, si, tr, ti):
    npc = 2
    big = pl.BlockSpec((npc, S5_ROWS, 128), lambda i: (i, 0, 0))
    small = pl.BlockSpec((npc, 8, 128), lambda i: (i, 0, 0))
    out = jax.ShapeDtypeStruct(sr.shape, f32)
    return pl.pallas_call(
        _s5scan_kernel,
        out_shape=(out, out),
        grid=(sr.shape[0] // npc,),
        in_specs=[big, big, small, small],
        out_specs=(big, big),
        compiler_params=pltpu.CompilerParams(vmem_limit_bytes=VMEM_LIMIT),
        name="s5_scan",
    )(sr, si, tr, ti)


def _s5c_kernel(u_ref, hr_ref, hi_ref, mt_ref, qc_ref, cp_ref, y_ref, acc_ref, w_scr, q_scr):
    b = pl.program_id(1)
    q = pl.program_id(2)
    gl = S5_G // S5_LB
    g2n = gl // S5_Q

    @pl.when(jnp.logical_and(b == 0, q == 0))
    def _():
        for g in range(gl):
            tmp = _dot(mt_ref[0, g], cp_ref[g]).astype(bf16)
            for s in range(S5_T):
                w_scr[s * 128 + g * S5_C:s * 128 + (g + 1) * S5_C, :] = tmp[s * S5_C:(s + 1) * S5_C, :]
        for qq in range(S5_Q):
            for g2 in range(g2n):
                tmp = _dot(qc_ref[qq, g2], cp_ref[g2n * qq + g2]).astype(bf16)
                for a in range(4):
                    r0 = a * g2n * S5_P + g2 * S5_P
                    q_scr[qq, r0:r0 + S5_P, :] = tmp[a * S5_P:(a + 1) * S5_P, :]

    parts = []
    for d in range(2):
        k = 4 * d + b
        parts.append(hr_ref[0, pl.ds(k, S5_NCH, stride=8), :])
        parts.append(hi_ref[0, pl.ds(k, S5_NCH, stride=8), :])
    hin = jnp.concatenate(parts, axis=-1).astype(bf16)
    contrib = _dot(hin, q_scr[q])

    @pl.when(q == 0)
    def _():
        acc_ref[...] = contrib + _dot(_s5_xcat(u_ref), w_scr[...])

    @pl.when(q > 0)
    def _():
        acc_ref[...] += contrib

    @pl.when(q == S5_Q - 1)
    def _():
        for t in range(S5_T):
            y_ref[0, pl.ds(t, S5_NCH, stride=S5_T), :] = acc_ref[:, 128 * t:128 * (t + 1)]


def _s5c(u, hr, hi, mt, qcat, cplace):
    nb = u.shape[0]
    hspec = pl.BlockSpec((1, S5_ROWS, 128), lambda l, b, q: (l * S5_Q + q, 0, 0))
    lanes = S5_T * 128
    return pl.pallas_call(
        _s5c_kernel,
        out_shape=jax.ShapeDtypeStruct((nb, NROW, D_S5), f32),
        grid=(S5_LB, nb, S5_Q),
        in_specs=[
            pl.BlockSpec((1, NROW, 128), lambda l, b, q: (b, 0, l)),
            hspec, hspec,
            pl.BlockSpec((1,) + mt.shape[1:], lambda l, b, q: (l, 0, 0, 0)),
            pl.BlockSpec((S5_Q,) + qcat.shape[1:], lambda l, b, q: (l, 0, 0, 0)),
            pl.BlockSpec(cplace.shape, lambda l, b, q: (0, 0, 0)),
        ],
        out_specs=pl.BlockSpec((1, NROW, 128), lambda l, b, q: (b, 0, l)),
        scratch_shapes=[pltpu.VMEM((S5_NCH, lanes), f32), pltpu.VMEM((lanes, lanes), bf16),
                        pltpu.VMEM((S5_Q, 4 * 128, lanes), bf16)],
        compiler_params=pltpu.CompilerParams(dimension_semantics=("arbitrary", "arbitrary", "arbitrary")),
        name="s5_read",
    )(u, hr, hi, mt, qcat, cplace)


def _s5_mix(u, mats, places):
    mt, pcat, qcat, tr, ti = mats
    cplace, cstate = places
    sr, si = _s5a(u, pcat, cstate)
    hr, hi = _s5scan(sr, si, tr, ti)
    return _s5c(u, hr, hi, mt, qcat, cplace)


FN_N1 = 8
FN_N2 = SEQ // FN_N1
FN_TILE = 16


def _fnet_consts():
    k = np.arange(FN_N2)
    ang = 2.0 * np.pi * np.outer(k, k) / FN_N2
    g = np.concatenate([np.cos(ang), -np.sin(ang)], axis=0) / np.sqrt(SEQ)
    r = np.arange(FN_N1)[:, None, None]
    kc = np.arange(FN_N2)[None, :, None]
    tw = 2.0 * np.pi * r * kc / SEQ * np.ones((1, 1, FN_HD))
    twc = np.cos(tw).reshape(SEQ, FN_HD)
    tws = np.sin(tw).reshape(SEQ, FN_HD)
    j = np.arange(FN_HD)
    angc = 2.0 * np.pi * np.outer(j, j) / FN_HD
    ch = np.concatenate([np.cos(angc), np.sin(angc)], axis=0) / np.sqrt(FN_HD)
    t = np.arange(CTX)
    angt = 2.0 * np.pi * np.outer(t, t) / CTX
    fc = np.concatenate([np.cos(angt), -np.sin(angt)], axis=0) / np.sqrt(CTX)
    as32 = lambda a: jnp.asarray(a.astype(np.float32))
    return as32(g), as32(twc), as32(tws), as32(ch), as32(fc)


def _dft8(a):
    h = 0.7071067811865476
    add = lambda p, q: (p[0] + q[0], p[1] + q[1])
    sub = lambda p, q: (p[0] - q[0], p[1] - q[1])
    mji = lambda p: (p[1], -p[0])
    w1 = lambda p: ((p[0] + p[1]) * h, (p[1] - p[0]) * h)
    w3 = lambda p: ((p[1] - p[0]) * h, -(p[0] + p[1]) * h)
    b0, b1 = add(a[0], a[4]), sub(a[0], a[4])
    b2, b3 = add(a[2], a[6]), sub(a[2], a[6])
    b4, b5 = add(a[1], a[5]), sub(a[1], a[5])
    b6, b7 = add(a[3], a[7]), sub(a[3], a[7])
    c0, c2 = add(b0, b2), sub(b0, b2)
    c1, c3 = add(b1, mji(b3)), sub(b1, mji(b3))
    d0, d2 = add(b4, b6), sub(b4, b6)
    d1, d3 = add(b5, mji(b7)), sub(b5, mji(b7))
    t1, t2, t3 = w1(d1), mji(d2), w3(d3)
    return [add(c0, d0), add(c1, t1), add(c2, t2), add(c3, t3),
            sub(c0, d0), sub(c1, t1), sub(c2, t2), sub(c3, t3)]


def _fnet_kernel(xa_ref, xb_ref, g_ref, twc_ref, tws_ref, ch_ref, fc_ref, o_ref, yr_ref, yi_ref):
    g = g_ref[...]
    chm = ch_ref[...]
    x_refs = (xa_ref, xb_ref)
    for r in range(FN_N1):
        xr = jnp.concatenate([x[0, pl.ds(r, FN_N2, stride=FN_N1), :] for x in x_refs], axis=-1).astype(bf16)
        y = _dot(g, xr)
        cs = twc_ref[FN_N2 * r:FN_N2 * (r + 1), :]
        sn = tws_ref[FN_N2 * r:FN_N2 * (r + 1), :]
        cs = jnp.concatenate([cs, cs], axis=-1)
        sn = jnp.concatenate([sn, sn], axis=-1)
        yr = y[0:FN_N2]
        yi = y[FN_N2:2 * FN_N2]
        yr_ref[r] = yr * cs + yi * sn
        yi_ref[r] = yi * cs - yr * sn

    def butterfly(i, carry):
        rows = pl.ds(pl.multiple_of(i * FN_TILE, FN_TILE), FN_TILE)
        for half in range(2):
            cols = slice(FN_HD * half, FN_HD * (half + 1))
            out = _dft8([(yr_ref[r, rows, cols], yi_ref[r, rows, cols]) for r in range(FN_N1)])
            for kr in range(FN_N1):
                yr_ref[kr, rows, cols] = out[kr][0]
                yi_ref[kr, rows, cols] = out[kr][1]
        return carry

    lax.fori_loop(0, FN_N2 // FN_TILE, butterfly, 0)

    for kr in range(FN_N1):
        for hd in range(2):
            cols = slice(FN_HD * hd, FN_HD * (hd + 1))
            v = jnp.concatenate([yr_ref[kr, :, cols], yi_ref[kr, :, cols]], axis=-1).astype(bf16)
            o_ref[0, FN_N2 * kr:FN_N2 * (kr + 1), cols] = _dot(v, chm).astype(o_ref.dtype)

    for hd in range(2):
        cols = slice(FN_HD * hd, FN_HD * (hd + 1))
        xc = x_refs[hd][0, SEQ:NROW, :].astype(bf16)
        vc = _dot(fc_ref[...], xc)
        v = jnp.concatenate([vc[0:CTX], vc[CTX:2 * CTX]], axis=-1).astype(bf16)
        o_ref[0, SEQ:NROW, cols] = _dot(v, chm).astype(o_ref.dtype)


def _fnet(fu, consts):
    nb = fu.shape[0]
    g, twc, tws, ch, fc = consts
    full = lambda a: pl.BlockSpec(a.shape, lambda b, h: (0, 0))
    return pl.pallas_call(
        _fnet_kernel,
        out_shape=jax.ShapeDtypeStruct((nb, NROW, D_FNET), bf16),
        grid=(nb, FN_HEADS // 2),
        in_specs=[pl.BlockSpec((1, NROW, FN_HD), lambda b, h: (b, 0, 2 * h)),
                  pl.BlockSpec((1, NROW, FN_HD), lambda b, h: (b, 0, 2 * h + 1)),
                  full(g), full(twc), full(tws), full(ch), full(fc)],
        out_specs=pl.BlockSpec((1, NROW, 2 * FN_HD), lambda b, h: (b, 0, h)),
        scratch_shapes=[pltpu.VMEM((FN_N1, FN_N2, 2 * FN_HD), f32), pltpu.VMEM((FN_N1, FN_N2, 2 * FN_HD), f32)],
        compiler_params=pltpu.CompilerParams(vmem_limit_bytes=VMEM_LIMIT),
        name="fnet",
    )(fu, fu, g, twc, tws, ch, fc)


def _out_kernel(x_ref, yf_ref, yb_ref, z_ref, ys_ref, u_ref, gs_ref, sp_ref, fg_ref,
                mod_ref, gn_ref, sd_ref, wg_ref, bg_ref, fw_ref, fb_ref, wo_ref, gp_ref, o_ref):
    b = pl.program_id(0)
    j = pl.program_id(1)
    row = jnp.where(j == NT - 1, pl.num_programs(0), b)
    gate = mod_ref[pl.ds(row, 1), 2 * D_MODEL:3 * D_MODEL]
    y = yf_ref[0].astype(f32) + yb_ref[0].astype(f32)
    v = y * _silu(z_ref[0].astype(f32))
    gw = D_SSD // GROUPS
    acc = None
    for g in range(GROUPS):
        vg = v[:, gw * g:gw * (g + 1)]
        vg = vg * lax.rsqrt(jnp.mean(vg * vg, axis=-1, keepdims=True) + EPS) * gn_ref[:, gw * g:gw * (g + 1)]
        part = _dot(vg.astype(bf16), wo_ref[gw * g:gw * (g + 1), :])
        acc = part if acc is None else acc + part
    ys = ys_ref[0] + sd_ref[...] * u_ref[0]
    gl = 0.5 * ys * (1.0 + jnp.tanh(0.7978845608028654 * (ys + 0.044715 * ys * ys * ys)))
    s5o = gl * _sigmoid(_dot(gl.astype(bf16), wg_ref[...]) + bg_ref[...]) * _silu(gs_ref[0].astype(f32))
    acc = acc + _dot(s5o.astype(bf16), wo_ref[D_SSD:D_SSD + D_S5, :])
    fo = (_dot(sp_ref[0].astype(bf16), fw_ref[...]) + fb_ref[...]) * _silu(fg_ref[0].astype(f32))
    acc = acc + _dot(fo.astype(bf16), wo_ref[D_SSD + D_S5:D_SSD + D_S5 + D_FNET, :])
    yn = acc * lax.rsqrt(jnp.mean(acc * acc, axis=-1, keepdims=True) + EPS) * gp_ref[...]
    o_ref[0] = x_ref[0] + gate * yn


def _out(x, yf, yb, z, ys, u, gs, sp, fg, mod_l, gn, sd, wg, bg, fw, fb, wo, gp, ntiles):
    nb = x.shape[0]
    row = lambda w: pl.BlockSpec((1, TM, w), lambda b, j: (b, j, 0))
    full = lambda a: pl.BlockSpec(a.shape, lambda b, j: (0, 0))
    return pl.pallas_call(
        _out_kernel,
        out_shape=jax.ShapeDtypeStruct((nb, ntiles * TM, D_MODEL), f32),
        grid=(nb, ntiles),
        in_specs=[row(D_MODEL), row(D_SSD), row(D_SSD), row(D_SSD), row(512), row(512), row(512),
                  row(512), row(512), full(mod_l), full(gn), full(sd), full(wg), full(bg),
                  full(fw), full(fb), full(wo), full(gp)],
        out_specs=row(D_MODEL),
        compiler_params=pltpu.CompilerParams(vmem_limit_bytes=VMEM_LIMIT),
        name="out",
    )(x, yf, yb, z, ys, u, gs, sp, fg, mod_l, gn, sd, wg, bg, fw, fb, wo, gp)


def _pack_w_in(w):
    z, xbc, dt, su, sg, fu, fg = jnp.split(w, (1024, 2560, 2592, 3104, 3616, 4128), axis=-1)
    dt = jnp.pad(dt, ((0, 0), (0, DT_PAD - 2 * HEADS)))
    return jnp.concatenate([z, xbc, su, sg, fu, fg, dt], axis=-1).astype(bf16)


def kernel(x, c, ctx, c_ctx, w_mod, b_mod, g_pre, g_post, w_in, conv_w, conv_b, dt_bias, a_log, d_ssd, g_ssd_norm, s5_lambda_re, s5_lambda_im, s5_log_step, s5_b_re, s5_b_im, s5_c_re, s5_c_im, s5_d, s5_w_glu, s5_b_glu, fnet_w, fnet_b, w_out):
    nb = x.shape[0]
    depth = w_in.shape[0]
    xs = jnp.concatenate([x, ctx], axis=1)
    c8 = jnp.concatenate([c, c_ctx[None, :], jnp.zeros((8 - nb - 1, D_MODEL), f32)], axis=0)
    mod = _mod_all(c8, w_mod, b_mod[:, None, :])
    expand = jnp.repeat(jnp.eye(HEADS, dtype=f32), HEADDIM, axis=1).astype(bf16)
    eye = jnp.eye(HEADS, dtype=f32).astype(bf16)
    fconsts = tuple(a.astype(bf16) if i in (0, 3, 4) else a for i, a in enumerate(_fnet_consts()))
    places = tuple(a.astype(bf16) for a in _s5_place_consts())
    for i in range(depth):
        z, xbc, u, gs, fu, fg, dtp = _inproj(xs, mod[i], g_pre[i][None, :], _pack_w_in(w_in[i]))
        act = _conv(xbc, conv_w[i].reshape(9, D_XBC), conv_b[i][None, :])
        yf, yb = _ssd(act, dtp, dt_bias[i], a_log[i], expand, eye, jnp.repeat(d_ssd[i], HEADDIM)[None, :])
        mats = _s5_matrices(s5_lambda_re[i], s5_lambda_im[i], s5_log_step[i], s5_b_re[i], s5_b_im[i],
                            s5_c_re[i], s5_c_im[i])
        ys = _s5_mix(u, mats, places)
        sp = _fnet(fu, fconsts)
        xs = _out(xs, yf, yb, z, ys, u, gs, sp, fg, mod[i], g_ssd_norm[i][None, :], s5_d[i][None, :],
                  s5_w_glu[i].astype(bf16), s5_b_glu[i][None, :], fnet_w[i].astype(bf16), fnet_b[i][None, :],
                  w_out[i].astype(bf16), g_post[i][None, :], NT if i < depth - 1 else NT - 1)
    return xs
```

```python
import functools

import numpy as np
import jax
import jax.numpy as jnp
from jax import lax
from jax.experimental import pallas as pl
from jax.experimental.pallas import tpu as pltpu

f32 = jnp.float32
bf16 = jnp.bfloat16

D_MODEL = 1024
SEQ = 4096
CTX = 256
NROW = SEQ + CTX
DEPTH = 4
GRID_W = 64
D_SSD = 1024
HEADDIM = 64
HEADS = 16
GROUPS = 2
HPG = HEADS // GROUPS
NSTATE = 128
CHUNK = 128
NCHUNK = NROW // CHUNK
NCHUNK_LAT = SEQ // CHUNK
D_XBC = D_SSD + 2 * GROUPS * NSTATE
D_S5 = 512
S5_G = 32
S5_C = 16
S5_P = 64
S5_T = 8
S5_NCH = NROW // S5_T
S5_NCH_LAT = SEQ // S5_T
S5_NCH_CTX = CTX // S5_T
S5_LB = 4
S5_Q = 4
S5_ROWS = S5_NCH * 8
D_FNET = 512
FN_HEADS = 4
FN_HD = 128
DT_PAD = 128
D_PACK = D_SSD + D_XBC + 4 * 512 + DT_PAD
EPS = 1e-6
TM = 256
NT = NROW // TM
VMEM_LIMIT = 56 * 1024 * 1024


def _sigmoid(v):
    return 1.0 / (1.0 + jnp.exp(-v))


def _silu(v):
    return v * _sigmoid(v)


def _split2(v):
    hi = v.astype(bf16)
    lo = (v - hi.astype(f32)).astype(bf16)
    return hi, lo


def _split3(v):
    hi = v.astype(bf16)
    r = v - hi.astype(f32)
    mid = r.astype(bf16)
    lo = (r - mid.astype(f32)).astype(bf16)
    return hi, mid, lo


def _dot(a, b):
    return jnp.dot(a, b, preferred_element_type=f32)


def _dot_nt(a, b):
    return lax.dot_general(a, b, (((1,), (1,)), ((), ())), preferred_element_type=f32)


def _mod_kernel(c_ref, w_ref, b_ref, o_ref):
    s = _silu(c_ref[...])
    o_ref[0] = _dot(s.astype(bf16), w_ref[0].astype(bf16)) + b_ref[0]


def _mod_all(c8, w_mod, b_mod):
    depth = w_mod.shape[0]
    return pl.pallas_call(
        _mod_kernel,
        out_shape=jax.ShapeDtypeStruct((depth, 8, 3 * D_MODEL), f32),
        grid=(depth, 3),
        in_specs=[
            pl.BlockSpec((8, D_MODEL), lambda l, n: (0, 0)),
            pl.BlockSpec((1, D_MODEL, D_MODEL), lambda l, n: (l, 0, n)),
            pl.BlockSpec((1, 1, D_MODEL), lambda l, n: (l, 0, n)),
        ],
        out_specs=pl.BlockSpec((1, 8, D_MODEL), lambda l, n: (l, 0, n)),
        name="mod",
    )(c8, w_mod, b_mod)


_PROJ_WIDTHS = (D_SSD, D_XBC, 512, 512, 512, 512, DT_PAD)
_PROJ_DTYPES = (bf16, bf16, f32, bf16, f32, bf16, f32)


def _inproj_kernel(x_ref, mod_ref, g_ref, w_ref, *out_refs):
    b = pl.program_id(0)
    j = pl.program_id(1)
    row = jnp.where(j == NT - 1, pl.num_programs(0), b)
    m = mod_ref[pl.ds(row, 1), :]
    shift = m[:, 0:D_MODEL]
    scale = m[:, D_MODEL:2 * D_MODEL]
    x = x_ref[0]
    ms = jnp.mean(x * x, axis=-1, keepdims=True)
    xn = x * lax.rsqrt(ms + EPS) * g_ref[...]
    h = (xn * (1.0 + scale) + shift).astype(bf16)
    off = 0
    for ref, width in zip(out_refs, _PROJ_WIDTHS):
        ref[0] = _dot(h, w_ref[:, off:off + width]).astype(ref.dtype)
        off += width


def _inproj(x, mod_l, g_pre, w_pack):
    nb = x.shape[0]
    outs = tuple(jax.ShapeDtypeStruct((nb, NROW, w), dt) for w, dt in zip(_PROJ_WIDTHS, _PROJ_DTYPES))
    return pl.pallas_call(
        _inproj_kernel,
        out_shape=outs,
        grid=(nb, NT),
        in_specs=[
            pl.BlockSpec((1, TM, D_MODEL), lambda b, j: (b, j, 0)),
            pl.BlockSpec((8, 3 * D_MODEL), lambda b, j: (0, 0)),
            pl.BlockSpec((1, D_MODEL), lambda b, j: (0, 0)),
            pl.BlockSpec((D_MODEL, D_PACK), lambda b, j: (0, 0)),
        ],
        out_specs=tuple(pl.BlockSpec((1, TM, w), lambda b, j: (b, j, 0)) for w in _PROJ_WIDTHS),
        compiler_params=pltpu.CompilerParams(vmem_limit_bytes=VMEM_LIMIT),
        name="inproj",
    )(x, mod_l, g_pre, w_pack)


def _conv_kernel(x_ref, w_ref, b_ref, o_ref, pad_ref, left_ref, right_ref):
    zero = jnp.zeros((GRID_W, 128), f32)
    for ref in (pad_ref, left_ref, right_ref):
        ref[0:GRID_W, :] = zero
        ref[GRID_W + SEQ:2 * GRID_W + SEQ, :] = zero
    w = w_ref[...]
    bias = b_ref[...]
    rid = lax.broadcasted_iota(jnp.int32, (GRID_W, 128), 0)
    first = rid == 0
    last = rid == GRID_W - 1

    def fill(r, carry):
        v = x_ref[0, pl.ds(pl.multiple_of(r * GRID_W, GRID_W), GRID_W), :].astype(f32)
        dst = pl.ds(pl.multiple_of((r + 1) * GRID_W, GRID_W), GRID_W)
        pad_ref[dst, :] = v
        left_ref[dst, :] = jnp.where(first, 0.0, pltpu.roll(v, 1, 0))
        right_ref[dst, :] = jnp.where(last, 0.0, pltpu.roll(v, GRID_W - 1, 0))
        return carry

    lax.fori_loop(0, SEQ // GRID_W, fill, 0)

    def body(r, carry):
        acc = jnp.zeros((GRID_W, 128), f32) + bias
        for i in range(3):
            rows = pl.ds(pl.multiple_of((r + i) * GRID_W, GRID_W), GRID_W)
            acc = (acc + w[3 * i:3 * i + 1] * left_ref[rows, :] + w[3 * i + 1:3 * i + 2] * pad_ref[rows, :]
                   + w[3 * i + 2:3 * i + 3] * right_ref[rows, :])
        o_ref[0, pl.ds(pl.multiple_of(r * GRID_W, GRID_W), GRID_W), :] = _silu(acc).astype(o_ref.dtype)
        return carry

    lax.fori_loop(0, SEQ // GRID_W, body, 0, unroll=2)

    xc = x_ref[0, SEQ:NROW, :].astype(f32)
    cid =lax.broadcasted_iota(jnp.int32, (CTX, 128), 0)
    vm = jnp.where(cid == 0, 0.0, pltpu.roll(xc, 1, 0))
    vp = jnp.where(cid == CTX - 1, 0.0, pltpu.roll(xc, CTX - 1, 0))
    acc = bias + w[3:4] * vm + w[4:5] * xc + w[5:6] * vp
    o_ref[0, SEQ:NROW, :] = _silu(acc).astype(o_ref.dtype)


def _conv(xbc, w9, cb):
    nb = xbc.shape[0]
    return pl.pallas_call(
        _conv_kernel,
        out_shape=jax.ShapeDtypeStruct((nb, NROW, D_XBC), bf16),
        grid=(nb, D_XBC // 128),
        in_specs=[
            pl.BlockSpec((1, NROW, 128), lambda b, c: (b, 0, c)),
            pl.BlockSpec((9, 128), lambda b, c: (0, c)),
            pl.BlockSpec((1, 128), lambda b, c: (0, c)),
        ],
        out_specs=pl.BlockSpec((1, NROW, 128), lambda b, c: (b, 0, c)),
        scratch_shapes=[pltpu.VMEM((SEQ + 2 * GRID_W, 128), f32)] * 3,
        name="conv",
    )(xbc, w9, cb)


def _ssd_kernel(xf_ref, bf_ref, cf_ref, dtf_ref, xb_ref, bb_ref, cb_ref, dtb_ref,
                dtbias_ref, alog_ref, e_ref, eye_ref, dsk_ref, yf_ref, yb_ref, h_ref, ybuf_ref):
    s = pl.program_id(1)

    @pl.when(s == 0)
    def _():
        h_ref[...] = jnp.zeros(h_ref.shape, f32)

    expand = e_ref[...]
    eye = eye_ref[...]
    ii = lax.broadcasted_iota(jnp.int32, (CHUNK, CHUNK), 0)
    jj = lax.broadcasted_iota(jnp.int32, (CHUNK, CHUNK), 1)
    low_head = (lax.broadcasted_iota(jnp.int32, (CHUNK, D_SSD), 1) & (2 * HEADDIM - 1)) < HEADDIM
    dirs =((xf_ref, bf_ref, cf_ref, dtf_ref, yf_ref), (xb_ref, bb_ref, cb_ref, dtb_ref, yb_ref))
    for d, (x_ref, b_ref, c_ref, dt_ref, y_ref) in enumerate(dirs):
        keep = (ii >= jj) if d == 0 else (ii <= jj)
        tri = keep.astype(bf16)
        x = x_ref[0].astype(f32)
        bm = b_ref[0].astype(f32)
        cm = c_ref[0]
        v = dt_ref[0][:, HEADS * d:HEADS * (d + 1)] + dtbias_ref[d:d + 1, :]
        dt = jnp.maximum(v, 0.0) + jnp.log1p(jnp.exp(-jnp.abs(v)))
        a = dt * (-jnp.exp(alog_ref[d:d + 1, :]))
        a3 = _split3(a)
        acum = _dot(tri, a3[0]) + _dot(tri, a3[1]) + _dot(tri, a3[2])
        total = acum[CHUNK - 1:CHUNK] if d == 0 else acum[0:1]
        c3 = _split3(acum)
        acum_t = _dot_nt(eye, c3[0]) + _dot_nt(eye, c3[1]) + _dot_nt(eye, c3[2])
        coarse = jnp.concatenate([dt, jnp.exp(total - acum)], axis=0).astype(bf16)
        ex = _dot(coarse, expand)
        dt_e = ex[0:CHUNK]
        dte_e = ex[CHUNK:2 * CHUNK]
        fine = jnp.concatenate([jnp.exp(acum), jnp.broadcast_to(jnp.exp(total), (8, HEADS))], axis=0)
        f_hi, f_lo = _split2(fine)
        ex = _dot(f_hi, expand) + _dot(f_lo, expand)
        ea_e = ex[0:CHUNK]
        cd_e = ex[CHUNK:CHUNK + 1]
        xdt = x * dt_e
        xdt_lo = jnp.where(low_head, xdt, 0.0).astype(bf16)
        xdt_hi = jnp.where(low_head, 0.0, xdt).astype(bf16)
        xw_b = (xdt * dte_e).astype(bf16)
        for g in range(GROUPS):
            bg = bm[:, NSTATE * g:NSTATE * (g + 1)]
            cg_b = cm[:, NSTATE * g:NSTATE * (g + 1)].astype(bf16)
            bg_t = bg.T.astype(bf16)
            cbm = _dot(cg_b, bg_t)
            hprev = h_ref[d, g]
            gl = slice(HPG * HEADDIM * g, HPG * HEADDIM * (g + 1))
            yoff = _dot(cg_b, hprev.astype(bf16)) * ea_e[:, gl]
            for pp in range(HPG // 2):
                gmats = []
                for h in (HPG * g + 2 * pp, HPG * g + 2 * pp + 1):
                    seg = acum[:, h:h + 1] - acum_t[h:h + 1, :]
                    dec = jnp.where(keep, jnp.exp(jnp.where(keep, seg, 0.0)), 0.0)
                    gmats.append((cbm * dec).astype(bf16))
                pair = slice(128 * (HPG // 2 * g + pp), 128 * (HPG // 2 * g + pp + 1))
                rhs = jnp.concatenate([xdt_lo[:, pair], xdt_hi[:, pair]], axis=0)
                yd = _dot(jnp.concatenate(gmats, axis=1), rhs)
                ybuf_ref[:, pair] = yd + yoff[:, 128 * pp:128 * (pp + 1)]
            st = _dot(bg_t, xw_b[:, gl])
            h_ref[d, g] = hprev * cd_e[:, gl] + st
        y = ybuf_ref[...]
        if d == 0:
            y = y + dsk_ref[...] * x
        y_ref[0] = y.astype(y_ref.dtype)


def _ssd(act, dtp, dt_bias, a_log, expand, eye, dsk):
    nb = act.shape[0]

    def fwd_chunk(s):
        return jnp.where(s < 2, NCHUNK_LAT + s, s - 2)

    def bwd_chunk(s):
        return NCHUNK - 1 - s

    def specs(chunk_fn):
        return [
            pl.BlockSpec((1, CHUNK, D_SSD), lambda b, s: (b, chunk_fn(s), 0)),
            pl.BlockSpec((1, CHUNK, 2 * NSTATE), lambda b, s: (b, chunk_fn(s), D_SSD // (2 * NSTATE))),
            pl.BlockSpec((1, CHUNK, 2 * NSTATE), lambda b, s: (b, chunk_fn(s), D_SSD // (2 * NSTATE) + 1)),
            pl.BlockSpec((1, CHUNK, DT_PAD), lambda b, s: (b, chunk_fn(s), 0)),
        ]

    out = jax.ShapeDtypeStruct((nb, NROW, D_SSD), bf16)
    return pl.pallas_call(
        _ssd_kernel,
        out_shape=(out, out),
        grid=(nb, NCHUNK),
        in_specs=specs(fwd_chunk) + specs(bwd_chunk) + [
            pl.BlockSpec((2, HEADS), lambda b, s: (0, 0)),
            pl.BlockSpec((2, HEADS), lambda b, s: (0, 0)),
            pl.BlockSpec((HEADS, D_SSD), lambda b, s: (0, 0)),
            pl.BlockSpec((HEADS, HEADS), lambda b, s: (0, 0)),
            pl.BlockSpec((1, D_SSD), lambda b, s: (0, 0)),
        ],
        out_specs=(
            pl.BlockSpec((1, CHUNK, D_SSD), lambda b, s: (b, fwd_chunk(s), 0)),
            pl.BlockSpec((1, CHUNK, D_SSD), lambda b, s: (b, bwd_chunk(s), 0)),
        ),
        scratch_shapes=[pltpu.VMEM((2, GROUPS, NSTATE, HPG * HEADDIM), f32), pltpu.VMEM((CHUNK, D_SSD), f32)],
        compiler_params=pltpu.CompilerParams(dimension_semantics=("arbitrary", "arbitrary")),
        name="ssd",
    )(act, act, act, dtp, act, act, act, dtp, dt_bias, a_log, expand, eye, dsk)


def _s5_matrices(lam_re, lam_im, log_step, b_re, b_im, c_re, c_im):
    hp = lax.Precision.HIGHEST
    t = S5_T
    lr = lam_re.astype(f32)
    li = lam_im.astype(f32)
    step = jnp.exp(log_step.astype(f32))[..., None]
    mag = jnp.exp(lr * step)
    ar = mag * jnp.cos(li * step)
    ai = mag * jnp.sin(li * step)
    den = lr * lr + li * li
    fr = ((ar - 1.0) * lr + ai * li) / den
    fi = (ai * lr - (ar - 1.0) * li) / den
    br = b_re.astype(f32)
    bi = b_im.astype(f32)
    bbr = fr[..., None] * br - fi[..., None] * bi
    bbi = fr[..., None] * bi + fi[..., None] * br
    taus = jnp.arange(t + 1, dtype=f32)[:, None]
    pmag = jnp.exp(taus * (lr * step)[:, :, None, :])
    pang = taus * (li * step)[:, :, None, :]
    pwr = pmag * jnp.cos(pang)
    pwi = pmag * jnp.sin(pang)
    cr = c_re.astype(f32)[:, :, None]
    ci = c_im.astype(f32)[:, :, None]
    dr = cr * pwr[:, :, :, None, :] - ci * pwi[:, :, :, None, :]
    di = cr * pwi[:, :, :, None, :] + ci * pwr[:, :, :, None, :]
    kern = (jnp.einsum('dgtkp,dgpj->dgtkj', dr, bbr, precision=hp)
            - jnp.einsum('dgtkp,dgpj->dgtkj', di, bbi, precision=hp))
    tt = jnp.arange(t)[:, None]
    ss = jnp.arange(t)[None, :]
    kf = kern[0][:, jnp.clip(tt - ss, 0, t)]
    kb = kern[1][:, jnp.clip(ss - tt, 0, t)]
    m = (jnp.where((tt >= ss)[None, :, :, None, None], kf, 0.0)
         + jnp.where((ss >= tt)[None, :, :, None, None], kb, 0.0))
    m_t = m.transpose(0, 2, 4, 1, 3).reshape(S5_G, t * S5_C, t * S5_C)
    ef = jnp.arange(t - 1, -1, -1)
    eb = jnp.arange(t)

    def state_map(d, exps):
        pr = pwr[d][:, exps]
        pi = pwi[d][:, exps]
        re = pr[:, :, None, :] * bbr[d].transpose(0, 2, 1)[:, None] - pi[:, :, None, :] * bbi[d].transpose(0, 2, 1)[:, None]
        im = pr[:, :, None, :] * bbi[d].transpose(0, 2, 1)[:, None] + pi[:, :, None, :] * bbr[d].transpose(0, 2, 1)[:, None]
        return re.reshape(S5_G, t * S5_C, S5_P), im.reshape(S5_G, t * S5_C, S5_P)

    pfr, pfi = state_map(0, ef)
    pbr, pbi = state_map(1, eb)

    def read_map(d, exps):
        qr = dr[d][:, exps].transpose(0, 3, 1, 2).reshape(S5_G, S5_P, t * S5_C)
        qi = -di[d][:, exps].transpose(0, 3, 1, 2).reshape(S5_G, S5_P, t * S5_C)
        return qr, qi

    qfr, qfi = read_map(0, jnp.arange(1, t + 1))
    qbr, qbi = read_map(1, jnp.arange(t, 0, -1))

    gl = S5_G // S5_LB
    npiece = S5_LB * S5_Q
    mt = m_t.reshape(S5_LB, gl, t * S5_C, t * S5_C)
    pcat = jnp.concatenate([pfr, pfi, pbr, pbi], axis=-1).reshape(npiece, gl // S5_Q, t * S5_C, 4 * S5_P)
    qcat = jnp.concatenate([qfr, qfi, qbr, qbi], axis=1).reshape(npiece, gl // S5_Q, 4 * S5_P, t * S5_C)
    tr = jnp.repeat(pwr[:, :, t, :].reshape(2, npiece, 1, 128), 4, axis=0).transpose(1, 0, 2, 3).reshape(npiece, 8, 128)
    ti = jnp.repeat(pwi[:, :, t, :].reshape(2, npiece, 1, 128), 4, axis=0).transpose(1, 0, 2, 3).reshape(npiece, 8, 128)
    return mt.astype(bf16), pcat.astype(bf16), qcat.astype(bf16), tr, ti


def _s5_place_consts():
    gl = S5_G // S5_LB
    g2 = gl // S5_Q
    cplace = np.zeros((gl, S5_T, S5_C, S5_T, gl, S5_C), np.float32)
    for g in range(gl):
        for t in range(S5_T):
            cplace[g, t, np.arange(S5_C), t, g, np.arange(S5_C)] = 1.0
    cstate = np.zeros((g2, 4, S5_P, 4, g2, S5_P), np.float32)
    for g in range(g2):
        for a in range(4):
            cstate[g, a, np.arange(S5_P), a, g, np.arange(S5_P)] = 1.0
    return (jnp.asarray(cplace.reshape(gl, S5_T * S5_C, S5_T * gl * S5_C)),
            jnp.asarray(cstate.reshape(g2, 4 * S5_P, 4 * g2 * S5_P)))


def _s5_xcat(u_ref):
    return jnp.concatenate([u_ref[0, pl.ds(t, S5_NCH, stride=S5_T), :] for t in range(S5_T)], axis=-1).astype(bf16)


def _s5a_kernel(u_ref, pc_ref, cs_ref, sr_ref, si_ref, p_scr):
    q = pl.program_id(1)
    b = pl.program_id(2)
    rows_g = S5_T * S5_C

    @pl.when(b == 0)
    def _():
        p_scr[...] = jnp.zeros(p_scr.shape, bf16)
        for g2 in range(S5_G // S5_LB // S5_Q):
            tmp = _dot(pc_ref[0, g2], cs_ref[g2]).astype(bf16)
            for s in range(S5_T):
                row = pl.multiple_of(s * 128 + (2 * q + g2) * S5_C, S5_C)
                p_scr[pl.ds(row, S5_C), :] = tmp[s * S5_C:(s + 1) * S5_C, :]

    r = _dot(_s5_xcat(u_ref), p_scr[...])
    for d in range(2):
        k = 4 * d + b
        sr_ref[0, pl.ds(k, S5_NCH, stride=8), :] = r[:, 256 * d:256 * d + 128]
        si_ref[0, pl.ds(k, S5_NCH, stride=8), :] = r[:, 256 * d + 128:256 * d + 256]


def _s5a(u, pcat, cstate):
    nb = u.shape[0]
    out = jax.ShapeDtypeStruct((S5_LB * S5_Q, S5_ROWS, 128), f32)
    ospec = pl.BlockSpec((1, S5_ROWS, 128), lambda l, q, b: (l * S5_Q + q, 0, 0))
    return pl.pallas_call(
        _s5a_kernel,
        out_shape=(out, out),
        grid=(S5_LB, S5_Q, nb),
        in_specs=[
            pl.BlockSpec((1, NROW, 128), lambda l, q, b: (b, 0, l)),
            pl.BlockSpec((1,) + pcat.shape[1:], lambda l, q, b: (l * S5_Q + q, 0, 0, 0)),
            pl.BlockSpec(cstate.shape, lambda l, q, b: (0, 0, 0)),
        ],
        out_specs=(ospec, ospec),
        scratch_shapes=[pltpu.VMEM((S5_T * 128, 512), bf16)],
        compiler_params=pltpu.CompilerParams(dimension_semantics=("arbitrary", "arbitrary", "arbitrary")),
        name="s5_state",
    )(u, pcat, cstate)


def _s5scan_kernel(sr_ref, si_ref, tr_ref, ti_ref, hr_ref, hi_ref):
    npc = sr_ref.shape[0]
    is_fwd = lax.broadcasted_iota(jnp.int32, (8, 128), 0) < 4
    is_bwd = jnp.logical_not(is_fwd)

    def body(s, carry):
        cf = jnp.where(s < S5_NCH_CTX, S5_NCH_LAT + s, s - S5_NCH_CTX)
        cb = S5_NCH - 1 - s
        rf = pl.multiple_of(cf * 8, 8)
        rb = pl.multiple_of(cb * 8, 8)
        new = []
        for i in range(npc):
            hr = carry[2 * i]
            hi = carry[2 * i + 1]
            xr = jnp.where(is_fwd, sr_ref[i, pl.ds(rf, 8), :], sr_ref[i, pl.ds(rb, 8), :])
            xi = jnp.where(is_fwd, si_ref[i, pl.ds(rf, 8), :], si_ref[i, pl.ds(rb, 8), :])
            pltpu.store(hr_ref.at[i, pl.ds(rf, 8), :], hr, mask=is_fwd)
            pltpu.store(hr_ref.at[i, pl.ds(rb, 8), :], hr, mask=is_bwd)
            pltpu.store(hi_ref.at[i, pl.ds(rf, 8), :], hi, mask=is_fwd)
            pltpu.store(hi_ref.at[i, pl.ds(rb, 8), :], hi, mask=is_bwd)
            tr = tr_ref[i]
            ti = ti_ref[i]
            new.append(tr * hr - ti * hi + xr)
            new.append(tr * hi + ti * hr + xi)
        return tuple(new)

    z = jnp.zeros((8, 128), f32)
    lax.fori_loop(0, S5_NCH, body, (z,) * (2 * npc))


def _s5scan(sr, si, tr, ti):
    npc = 2
    big = pl.BlockSpec((npc, S5_ROWS, 128), lambda i: (i, 0, 0))
    small = pl.BlockSpec((npc, 8, 128), lambda i: (i, 0, 0))
    out = jax.ShapeDtypeStruct(sr.shape, f32)
    return pl.pallas_call(
        _s5scan_kernel,
        out_shape=(out, out),
        grid=(sr.shape[0] // npc,),
        in_specs=[big, big, small, small],
        out_specs=(big, big),
        compiler_params=pltpu.CompilerParams(vmem_limit_bytes=VMEM_LIMIT),
        name="s5_scan",
    )(sr, si, tr, ti)


def _s5c_kernel(u_ref, hr_ref, hi_ref, mt_ref, qc_ref, cp_ref, y_ref, acc_ref, w_scr, q_scr):
    b = pl.program_id(1)
    q = pl.program_id(2)
    gl = S5_G // S5_LB
    g2n = gl // S5_Q

    @pl.when(jnp.logical_and(b == 0, q == 0))
    def _():
        for g in range(gl):
            tmp = _dot(mt_ref[0, g], cp_ref[g]).astype(bf16)
            for s in range(S5_T):
                w_scr[s * 128 + g * S5_C:s * 128 + (g + 1) * S5_C, :] = tmp[s * S5_C:(s + 1) * S5_C, :]
        for qq in range(S5_Q):
            for g2 in range(g2n):
                tmp = _dot(qc_ref[qq, g2], cp_ref[g2n * qq + g2]).astype(bf16)
                for a in range(4):
                    r0 = a * g2n * S5_P + g2 * S5_P
                    q_scr[qq, r0:r0 + S5_P, :] = tmp[a * S5_P:(a + 1) * S5_P, :]

    parts = []
    for d in range(2):
        k = 4 * d + b
        parts.append(hr_ref[0, pl.ds(k, S5_NCH, stride=8), :])
        parts.append(hi_ref[0, pl.ds(k, S5_NCH, stride=8), :])
    hin = jnp.concatenate(parts, axis=-1).astype(bf16)
    contrib = _dot(hin, q_scr[q])

    @pl.when(q == 0)
    def _():
        acc_ref[...] = contrib + _dot(_s5_xcat(u_ref), w_scr[...])

    @pl.when(q > 0)
    def _():
        acc_ref[...] += contrib

    @pl.when(q == S5_Q - 1)
    def _():
        for t in range(S5_T):
            y_ref[0, pl.ds(t, S5_NCH, stride=S5_T), :] = acc_ref[:, 128 * t:128 * (t + 1)]


def _s5c(u, hr, hi, mt, qcat, cplace):
    nb = u.shape[0]
    hspec = pl.BlockSpec((1, S5_ROWS, 128), lambda l, b, q: (l * S5_Q + q, 0, 0))
    lanes = S5_T * 128
    return pl.pallas_call(
        _s5c_kernel,
        out_shape=jax.ShapeDtypeStruct((nb, NROW, D_S5), f32),
        grid=(S5_LB, nb, S5_Q),
        in_specs=[
            pl.BlockSpec((1, NROW, 128), lambda l, b, q: (b, 0, l)),
            hspec, hspec,
            pl.BlockSpec((1,) + mt.shape[1:], lambda l, b, q: (l, 0, 0, 0)),
            pl.BlockSpec((S5_Q,) + qcat.shape[1:], lambda l, b, q: (l, 0, 0, 0)),
            pl.BlockSpec(cplace.shape, lambda l, b, q: (0, 0, 0)),
        ],
        out_specs=pl.BlockSpec((1, NROW, 128), lambda l, b, q: (b, 0, l)),
        scratch_shapes=[pltpu.VMEM((S5_NCH, lanes), f32), pltpu.VMEM((lanes, lanes), bf16),
                        pltpu.VMEM((S5_Q, 4 * 128, lanes), bf16)],
        compiler_params=pltpu.CompilerParams(dimension_semantics=("arbitrary", "arbitrary", "arbitrary")),
        name="s5_read",
    )(u, hr, hi, mt, qcat, cplace)


def _s5_mix(u, mats, places):
    mt, pcat, qcat, tr, ti = mats
    cplace, cstate = places
    sr, si = _s5a(u, pcat, cstate)
    hr, hi = _s5scan(sr, si, tr, ti)
    return _s5c(u, hr, hi, mt, qcat, cplace)


FN_N1 = 8
FN_N2 = SEQ // FN_N1
FN_TILE = 16


def _fnet_consts():
    k = np.arange(FN_N2)
    ang = 2.0 * np.pi * np.outer(k, k) / FN_N2
    g = np.concatenate([np.cos(ang), -np.sin(ang)], axis=0) / np.sqrt(SEQ)
    r = np.arange(FN_N1)[:, None, None]
    kc = np.arange(FN_N2)[None, :, None]
    tw = 2.0 * np.pi * r * kc / SEQ * np.ones((1, 1, FN_HD))
    twc = np.cos(tw).reshape(SEQ, FN_HD)
    tws = np.sin(tw).reshape(SEQ, FN_HD)
    j = np.arange(FN_HD)
    angc = 2.0 * np.pi * np.outer(j, j) / FN_HD
    ch = np.concatenate([np.cos(angc), np.sin(angc)], axis=0) / np.sqrt(FN_HD)
    t = np.arange(CTX)
    angt = 2.0 * np.pi * np.outer(t, t) / CTX
    fc = np.concatenate([np.cos(angt), -np.sin(angt)], axis=0) / np.sqrt(CTX)
    as32 = lambda a: jnp.asarray(a.astype(np.float32))
    return as32(g), as32(twc), as32(tws), as32(ch), as32(fc)


def _dft8(a):
    h = 0.7071067811865476
    add = lambda p, q: (p[0] + q[0], p[1] + q[1])
    sub = lambda p, q: (p[0] - q[0], p[1] - q[1])
    mji = lambda p: (p[1], -p[0])
    w1 = lambda p: ((p[0] + p[1]) * h, (p[1] - p[0]) * h)
    w3 = lambda p: ((p[1] - p[0]) * h, -(p[0] + p[1]) * h)
    b0, b1 = add(a[0], a[4]), sub(a[0], a[4])
    b2, b3 = add(a[2], a[6]), sub(a[2], a[6])
    b4, b5 = add(a[1], a[5]), sub(a[1], a[5])
    b6, b7 = add(a[3], a[7]), sub(a[3], a[7])
    c0, c2 = add(b0, b2), sub(b0, b2)
    c1, c3 = add(b1, mji(b3)), sub(b1, mji(b3))
    d0, d2 = add(b4, b6), sub(b4, b6)
    d1, d3 = add(b5, mji(b7)), sub(b5, mji(b7))
    t1, t2, t3 = w1(d1), mji(d2), w3(d3)
    return [add(c0, d0), add(c1, t1), add(c2, t2), add(c3, t3),
            sub(c0, d0), sub(c1, t1), sub(c2, t2), sub(c3, t3)]


def _fnet_kernel(xa_ref, xb_ref, g_ref, twc_ref, tws_ref, ch_ref, fc_ref, o_ref, yr_ref, yi_ref):
    g = g_ref[...]
    chm = ch_ref[...]
    x_refs = (xa_ref, xb_ref)
    for r in range(FN_N1):
        xr = jnp.concatenate([x[0, pl.ds(r, FN_N2, stride=FN_N1), :] for x in x_refs], axis=-1).astype(bf16)
        y = _dot(g, xr)
        cs = twc_ref[FN_N2 * r:FN_N2 * (r + 1), :]
        sn = tws_ref[FN_N2 * r:FN_N2 * (r + 1), :]
        cs = jnp.concatenate([cs, cs], axis=-1)
        sn = jnp.concatenate([sn, sn], axis=-1)
        yr = y[0:FN_N2]
        yi = y[FN_N2:2 * FN_N2]
        yr_ref[r] = yr * cs + yi * sn
        yi_ref[r] = yi * cs - yr * sn

    def butterfly(i, carry):
        rows = pl.ds(pl.multiple_of(i * FN_TILE, FN_TILE), FN_TILE)
        for half in range(2):
            cols = slice(FN_HD * half, FN_HD * (half + 1))
            out = _dft8([(yr_ref[r, rows, cols], yi_ref[r, rows, cols]) for r in range(FN_N1)])
            for kr in range(FN_N1):
                yr_ref[kr, rows, cols] = out[kr][0]
                yi_ref[kr, rows, cols] = out[kr][1]
        return carry

    lax.fori_loop(0, FN_N2 // FN_TILE, butterfly, 0)

    for kr in range(FN_N1):
        for hd in range(2):
            cols = slice(FN_HD * hd, FN_HD * (hd + 1))
            v = jnp.concatenate([yr_ref[kr, :, cols], yi_ref[kr, :, cols]], axis=-1).astype(bf16)
            o_ref[0, FN_N2 * kr:FN_N2 * (kr + 1), cols] = _dot(v, chm).astype(o_ref.dtype)

    for hd in range(2):
        cols = slice(FN_HD * hd, FN_HD * (hd + 1))
        xc = x_refs[hd][0, SEQ:NROW, :].astype(bf16)
        vc = _dot(fc_ref[...], xc)
        v = jnp.concatenate([vc[0:CTX], vc[CTX:2 * CTX]], axis=-1).astype(bf16)
        o_ref[0, SEQ:NROW, cols] = _dot(v, chm).astype(o_ref.dtype)


def _fnet(fu, consts):
    nb = fu.shape[0]
    g, twc, tws, ch, fc = consts
    full = lambda a: pl.BlockSpec(a.shape, lambda b, h: (0, 0))
    return pl.pallas_call(
        _fnet_kernel,
        out_shape=jax.ShapeDtypeStruct((nb, NROW, D_FNET), bf16),
        grid=(nb, FN_HEADS // 2),
        in_specs=[pl.BlockSpec((1, NROW, FN_HD), lambda b, h: (b, 0, 2 * h)),
                  pl.BlockSpec((1, NROW, FN_HD), lambda b, h: (b, 0, 2 * h + 1)),
                  full(g), full(twc), full(tws), full(ch), full(fc)],
        out_specs=pl.BlockSpec((1, NROW, 2 * FN_HD), lambda b, h: (b, 0, h)),
        scratch_shapes=[pltpu.VMEM((FN_N1, FN_N2, 2 * FN_HD), f32), pltpu.VMEM((FN_N1, FN_N2, 2 * FN_HD), f32)],
        compiler_params=pltpu.CompilerParams(vmem_limit_bytes=VMEM_LIMIT),
        name="fnet",
    )(fu, fu, g, twc, tws, ch, fc)


def _out_kernel(x_ref, yf_ref, yb_ref, z_ref, ys_ref, u_ref, gs_ref, sp_ref, fg_ref,
                mod_ref, gn_ref, sd_ref, wg_ref, bg_ref, fw_ref, fb_ref, wo_ref, gp_ref, o_ref):
    b = pl.program_id(0)
    j = pl.program_id(1)
    row = jnp.where(j == NT - 1, pl.num_programs(0), b)
    gate = mod_ref[pl.ds(row, 1), 2 * D_MODEL:3 * D_MODEL]
    y = yf_ref[0].astype(f32) + yb_ref[0].astype(f32)
    v = y * _silu(z_ref[0].astype(f32))
    gw = D_SSD // GROUPS
    acc = None
    for g in range(GROUPS):
        vg = v[:, gw * g:gw * (g + 1)]
        vg = vg * lax.rsqrt(jnp.mean(vg * vg, axis=-1, keepdims=True) + EPS) * gn_ref[:, gw * g:gw * (g + 1)]
        part = _dot(vg.astype(bf16), wo_ref[gw * g:gw * (g + 1), :])
        acc = part if acc is None else acc + part
    ys = ys_ref[0] + sd_ref[...] * u_ref[0]
    gl = 0.5 * ys * (1.0 + jnp.tanh(0.7978845608028654 * (ys + 0.044715 * ys * ys * ys)))
    s5o = gl * _sigmoid(_dot(gl.astype(bf16), wg_ref[...]) + bg_ref[...]) * _silu(gs_ref[0].astype(f32))
    acc = acc + _dot(s5o.astype(bf16), wo_ref[D_SSD:D_SSD + D_S5, :])
    fo = (_dot(sp_ref[0].astype(bf16), fw_ref[...]) + fb_ref[...]) * _silu(fg_ref[0].astype(f32))
    acc = acc + _dot(fo.astype(bf16), wo_ref[D_SSD + D_S5:D_SSD + D_S5 + D_FNET, :])
    yn = acc * lax.rsqrt(jnp.mean(acc * acc, axis=-1, keepdims=True) + EPS) * gp_ref[...]
    o_ref[0] = x_ref[0] + gate * yn


def _out(x, yf, yb, z, ys, u, gs, sp, fg, mod_l, gn, sd, wg, bg, fw, fb, wo, gp, ntiles):
    nb = x.shape[0]
    row = lambda w: pl.BlockSpec((1, TM, w), lambda b, j: (b, j, 0))
    full = lambda a: pl.BlockSpec(a.shape, lambda b, j: (0, 0))
    return pl.pallas_call(
        _out_kernel,
        out_shape=jax.ShapeDtypeStruct((nb, ntiles * TM, D_MODEL), f32),
        grid=(nb, ntiles),
        in_specs=[row(D_MODEL), row(D_SSD), row(D_SSD), row(D_SSD), row(512), row(512), row(512),
                  row(512), row(512), full(mod_l), full(gn), full(sd), full(wg), full(bg),
                  full(fw), full(fb), full(wo), full(gp)],
        out_specs=row(D_MODEL),
        compiler_params=pltpu.CompilerParams(vmem_limit_bytes=VMEM_LIMIT),
        name="out",
    )(x, yf, yb, z, ys, u, gs, sp, fg, mod_l, gn, sd, wg, bg, fw, fb, wo, gp)


def _pack_w_in(w):
    z, xbc, dt, su, sg, fu, fg = jnp.split(w, (1024, 2560, 2592, 3104, 3616, 4128), axis=-1)
    dt = jnp.pad(dt, ((0, 0), (0, DT_PAD - 2 * HEADS)))
    return jnp.concatenate([z, xbc, su, sg, fu, fg, dt], axis=-1).astype(bf16)


def kernel(x, c, ctx, c_ctx, w_mod, b_mod, g_pre, g_post, w_in, conv_w, conv_b, dt_bias, a_log, d_ssd, g_ssd_norm, s5_lambda_re, s5_lambda_im, s5_log_step, s5_b_re, s5_b_im, s5_c_re, s5_c_im, s5_d, s5_w_glu, s5_b_glu, fnet_w, fnet_b, w_out):
    nb = x.shape[0]
    depth = w_in.shape[0]
    xs = jnp.concatenate([x, ctx], axis=1)
    c8 = jnp.concatenate([c, c_ctx[None, :], jnp.zeros((8 - nb - 1, D_MODEL), f32)], axis=0)
    mod = _mod_all(c8, w_mod, b_mod[:, None, :])
    expand = jnp.repeat(jnp.eye(HEADS, dtype=f32), HEADDIM, axis=1).astype(bf16)
    eye = jnp.eye(HEADS, dtype=f32).astype(bf16)
    fconsts = tuple(a.astype(bf16) if i in (0, 3, 4) else a for i, a in enumerate(_fnet_consts()))
    places = tuple(a.astype(bf16) for a in _s5_place_consts())
    for i in range(depth):
        z, xbc, u, gs, fu, fg, dtp = _inproj(xs, mod[i], g_pre[i][None, :], _pack_w_in(w_in[i]))
        act = _conv(xbc, conv_w[i].reshape(9, D_XBC), conv_b[i][None, :])
        yf, yb = _ssd(act, dtp, dt_bias[i], a_log[i], expand, eye, jnp.repeat(d_ssd[i], HEADDIM)[None, :])
        mats = _s5_matrices(s5_lambda_re[i], s5_lambda_im[i], s5_log_step[i], s5_b_re[i], s5_b_im[i],
                            s5_c_re[i], s5_c_im[i])
        ys = _s5_mix(u, mats, places)
        sp = _fnet(fu, fconsts)
        xs = _out(xs, yf, yb, z, ys, u, gs, sp, fg, mod[i], g_ssd_norm[i][None, :], s5_d[i][None, :],
                  s5_w_glu[i].astype(bf16), s5_b_glu[i][None, :], fnet_w[i].astype(bf16), fnet_b[i][None, :],
                  w_out[i].astype(bf16), g_post[i][None, :], NT if i < depth - 1 else NT - 1)
    return xs
```

```python
import functools

import numpy as np
import jax
import jax.numpy as jnp
from jax import lax
from jax.experimental import pallas as pl
from jax.experimental.pallas import tpu as pltpu

f32 = jnp.float32
bf16 = jnp.bfloat16

D_MODEL = 1024
SEQ = 4096
CTX = 256
NROW = SEQ + CTX
DEPTH = 4
GRID_W = 64
D_SSD = 1024
HEADDIM = 64
HEADS = 16
GROUPS = 2
HPG = HEADS // GROUPS
NSTATE = 128
CHUNK = 128
NCHUNK = NROW // CHUNK
NCHUNK_LAT = SEQ // CHUNK
D_XBC = D_SSD + 2 * GROUPS * NSTATE
D_S5 = 512
S5_G = 32
S5_C = 16
S5_P = 64
S5_T = 8
S5_NCH = NROW // S5_T
S5_NCH_LAT = SEQ // S5_T
S5_NCH_CTX = CTX // S5_T
S5_LB = 4
S5_Q = 4
S5_ROWS = S5_NCH * 8
D_FNET = 512
FN_HEADS = 4
FN_HD = 128
DT_PAD = 128
D_PACK = D_SSD + D_XBC + 4 * 512 + DT_PAD
EPS = 1e-6
TM = 256
NT = NROW // TM
VMEM_LIMIT = 56 * 1024 * 1024


def _sigmoid(v):
    return 0.5 + 0.5 * jnp.tanh(0.5 * v)


def _silu(v):
    half = 0.5 * v
    return half + half * jnp.tanh(half)


def _split2(v):
    hi = v.astype(bf16)
    lo = (v - hi.astype(f32)).astype(bf16)
    return hi, lo


def _split3(v):
    hi = v.astype(bf16)
    r = v - hi.astype(f32)
    mid = r.astype(bf16)
    lo = (r - mid.astype(f32)).astype(bf16)
    return hi, mid, lo


def _dot(a, b):
    return jnp.dot(a, b, preferred_element_type=f32)


def _dot_nt(a, b):
    return lax.dot_general(a, b, (((1,), (1,)), ((), ())), preferred_element_type=f32)


def _mod_kernel(c_ref, w_ref, b_ref, o_ref):
    s = _silu(c_ref[...])
    o_ref[0] = _dot(s.astype(bf16), w_ref[0].astype(bf16)) + b_ref[0]


def _mod_all(c8, w_mod, b_mod):
    depth = w_mod.shape[0]
    return pl.pallas_call(
        _mod_kernel,
        out_shape=jax.ShapeDtypeStruct((depth, 8, 3 * D_MODEL), f32),
        grid=(depth, 3),
        in_specs=[
            pl.BlockSpec((8, D_MODEL), lambda l, n: (0, 0)),
            pl.BlockSpec((1, D_MODEL, D_MODEL), lambda l, n: (l, 0, n)),
            pl.BlockSpec((1, 1, D_MODEL), lambda l, n: (l, 0, n)),
        ],
        out_specs=pl.BlockSpec((1, 8, D_MODEL), lambda l, n: (l, 0, n)),
        name="mod",
    )(c8, w_mod, b_mod)


_PROJ_WIDTHS = (D_SSD, D_XBC, 512, 512, 512, 512, DT_PAD)
_PROJ_DTYPES = (bf16, bf16, f32, bf16, f32, bf16, f32)


def _inproj_kernel(x_ref, mod_ref, g_ref, w_ref, *out_refs):
    b = pl.program_id(0)
    j = pl.program_id(1)
    row = jnp.where(j == NT - 1, pl.num_programs(0), b)
    m = mod_ref[pl.ds(row, 1), :]
    shift = m[:, 0:D_MODEL]
    scale = m[:, D_MODEL:2 * D_MODEL]
    x = x_ref[0]
    ms = jnp.mean(x * x, axis=-1, keepdims=True)
    xn = x * lax.rsqrt(ms + EPS) * g_ref[...]
    h = (xn * (1.0 + scale) + shift).astype(bf16)
    off = 0
    for ref, width in zip(out_refs, _PROJ_WIDTHS):
        ref[0] = _dot(h, w_ref[:, off:off + width]).astype(ref.dtype)
        off += width


def _inproj(x, mod_l, g_pre, w_pack):
    nb = x.shape[0]
    outs = tuple(jax.ShapeDtypeStruct((nb, NROW, w), dt) for w, dt in zip(_PROJ_WIDTHS, _PROJ_DTYPES))
    return pl.pallas_call(
        _inproj_kernel,
        out_shape=outs,
        grid=(nb, NT),
        in_specs=[
            pl.BlockSpec((1, TM, D_MODEL), lambda b, j: (b, j, 0)),
            pl.BlockSpec((8, 3 * D_MODEL), lambda b, j: (0, 0)),
            pl.BlockSpec((1, D_MODEL), lambda b, j: (0, 0)),
            pl.BlockSpec((D_MODEL, D_PACK), lambda b, j: (0, 0)),
        ],
        out_specs=tuple(pl.BlockSpec((1, TM, w), lambda b, j: (b, j, 0)) for w in _PROJ_WIDTHS),
        compiler_params=pltpu.CompilerParams(vmem_limit_bytes=VMEM_LIMIT),
        name="inproj",
    )(x, mod_l, g_pre, w_pack)


def _conv_kernel(x_ref, w_ref, b_ref, o_ref, pad_ref, left_ref, right_ref):
    zero = jnp.zeros((GRID_W, 128), f32)
    for ref in (pad_ref, left_ref, right_ref):
        ref[0:GRID_W, :] = zero
        ref[GRID_W + SEQ:2 * GRID_W + SEQ, :] = zero
    w = w_ref[...]
    bias = b_ref[...]
    rid = lax.broadcasted_iota(jnp.int32, (GRID_W, 128), 0)
    first = rid == 0
    last = rid == GRID_W - 1

    def fill(r, carry):
        v = x_ref[0, pl.ds(pl.multiple_of(r * GRID_W, GRID_W), GRID_W), :].astype(f32)
        dst = pl.ds(pl.multiple_of((r + 1) * GRID_W, GRID_W), GRID_W)
        pad_ref[dst, :] = v
        left_ref[dst, :] = jnp.where(first, 0.0, pltpu.roll(v, 1, 0))
        right_ref[dst, :] = jnp.where(last, 0.0, pltpu.roll(v, GRID_W - 1, 0))
        return carry

    lax.fori_loop(0, SEQ // GRID_W, fill, 0)

    def body(r, carry):
        acc = jnp.zeros((GRID_W, 128), f32) + bias
        for i in range(3):
            rows = pl.ds(pl.multiple_of((r + i) * GRID_W, GRID_W), GRID_W)
            acc = (acc + w[3 * i:3 * i + 1] * left_ref[rows, :] + w[3 * i + 1:3 * i + 2] * pad_ref[rows, :]
                   + w[3 * i + 2:3 * i + 3] * right_ref[rows, :])
        o_ref[0, pl.ds(pl.multiple_of(r * GRID_W, GRID_W), GRID_W), :] = _silu(acc).astype(o_ref.dtype)
        return carry

    lax.fori_loop(0, SEQ // GRID_W, body, 0, unroll=2)

    xc = x_ref[0, SEQ:NROW, :].astype(f32)
    cid =lax.broadcasted_iota(jnp.int32, (CTX, 128), 0)
    vm = jnp.where(cid == 0, 0.0, pltpu.roll(xc, 1, 0))
    vp = jnp.where(cid == CTX - 1, 0.0, pltpu.roll(xc, CTX - 1, 0))
    acc = bias + w[3:4] * vm + w[4:5] * xc + w[5:6] * vp
    o_ref[0, SEQ:NROW, :] = _silu(acc).astype(o_ref.dtype)


def _conv(xbc, w9, cb):
    nb = xbc.shape[0]
    return pl.pallas_call(
        _conv_kernel,
        out_shape=jax.ShapeDtypeStruct((nb, NROW, D_XBC), bf16),
        grid=(nb, D_XBC // 128),
        in_specs=[
            pl.BlockSpec((1, NROW, 128), lambda b, c: (b, 0, c)),
            pl.BlockSpec((9, 128), lambda b, c: (0, c)),
            pl.BlockSpec((1, 128), lambda b, c: (0, c)),
        ],
        out_specs=pl.BlockSpec((1, NROW, 128), lambda b, c: (b, 0, c)),
        scratch_shapes=[pltpu.VMEM((SEQ + 2 * GRID_W, 128), f32)] * 3,
        name="conv",
    )(xbc, w9, cb)


def _ssd_kernel(xf_ref, bf_ref, cf_ref, dtf_ref, xb_ref, bb_ref, cb_ref, dtb_ref,
                dtbias_ref, alog_ref, e_ref, eye_ref, dsk_ref, yf_ref, yb_ref, h_ref, ybuf_ref):
    s = pl.program_id(1)

    @pl.when(s == 0)
    def _():
        h_ref[...] = jnp.zeros(h_ref.shape, f32)

    expand = e_ref[...]
    eye = eye_ref[...]
    ii = lax.broadcasted_iota(jnp.int32, (CHUNK, CHUNK), 0)
    jj = lax.broadcasted_iota(jnp.int32, (CHUNK, CHUNK), 1)
    low_head = (lax.broadcasted_iota(jnp.int32, (CHUNK, D_SSD), 1) & (2 * HEADDIM - 1)) < HEADDIM
    dirs =((xf_ref, bf_ref, cf_ref, dtf_ref, yf_ref), (xb_ref, bb_ref, cb_ref, dtb_ref, yb_ref))
    for d, (x_ref, b_ref, c_ref, dt_ref, y_ref) in enumerate(dirs):
        keep = (ii >= jj) if d == 0 else (ii <= jj)
        tri = keep.astype(bf16)
        x = x_ref[0].astype(f32)
        bm = b_ref[0].astype(f32)
        cm = c_ref[0]
        v = dt_ref[0][:, HEADS * d:HEADS * (d + 1)] + dtbias_ref[d:d + 1, :]
        dt = jnp.maximum(v, 0.0) + jnp.log1p(jnp.exp(-jnp.abs(v)))
        a = dt * (-jnp.exp(alog_ref[d:d + 1, :]))
        a3 = _split3(a)
        acum = _dot(tri, a3[0]) + _dot(tri, a3[1]) + _dot(tri, a3[2])
        total = acum[CHUNK - 1:CHUNK] if d == 0 else acum[0:1]
        c3 = _split3(acum)
        acum_t = _dot_nt(eye, c3[0]) + _dot_nt(eye, c3[1]) + _dot_nt(eye, c3[2])
        coarse = jnp.concatenate([dt, jnp.exp(total - acum)], axis=0).astype(bf16)
        ex = _dot(coarse, expand)
        dt_e = ex[0:CHUNK]
        dte_e = ex[CHUNK:2 * CHUNK]
        fine = jnp.concatenate([jnp.exp(acum), jnp.broadcast_to(jnp.exp(total), (8, HEADS))], axis=0)
        f_hi, f_lo = _split2(fine)
        ex = _dot(f_hi, expand) + _dot(f_lo, expand)
        ea_e = ex[0:CHUNK]
        cd_e = ex[CHUNK:CHUNK + 1]
        xdt = x * dt_e
        xdt_lo = jnp.where(low_head, xdt, 0.0).astype(bf16)
        xdt_hi = jnp.where(low_head, 0.0, xdt).astype(bf16)
        xw_b = (xdt * dte_e).astype(bf16)
        for g in range(GROUPS):
            bg = bm[:, NSTATE * g:NSTATE * (g + 1)]
            cg_b = cm[:, NSTATE * g:NSTATE * (g + 1)].astype(bf16)
            bg_t = bg.T.astype(bf16)
            cbm = _dot(cg_b, bg_t)
            hprev = h_ref[d, g]
            gl = slice(HPG * HEADDIM * g, HPG * HEADDIM * (g + 1))
            yoff = _dot(cg_b, hprev.astype(bf16)) * ea_e[:, gl]
            for pp in range(HPG // 2):
                gmats = []
                for h in (HPG * g + 2 * pp, HPG * g + 2 * pp + 1):
                    seg = acum[:, h:h + 1] - acum_t[h:h + 1, :]
                    dec = jnp.where(keep, jnp.exp(jnp.where(keep, seg, 0.0)), 0.0)
                    gmats.append((cbm * dec).astype(bf16))
                pair = slice(128 * (HPG // 2 * g + pp), 128 * (HPG // 2 * g + pp + 1))
                rhs = jnp.concatenate([xdt_lo[:, pair], xdt_hi[:, pair]], axis=0)
                yd = _dot(jnp.concatenate(gmats, axis=1), rhs)
                ybuf_ref[:, pair] = yd + yoff[:, 128 * pp:128 * (pp + 1)]
            st = _dot(bg_t, xw_b[:, gl])
            h_ref[d, g] = hprev * cd_e[:, gl] + st
        y = ybuf_ref[...]
        if d == 0:
            y = y + dsk_ref[...] * x
        y_ref[0] = y.astype(y_ref.dtype)


def _ssd(act, dtp, dt_bias, a_log, expand, eye, dsk):
    nb = act.shape[0]

    def fwd_chunk(s):
        return jnp.where(s < 2, NCHUNK_LAT + s, s - 2)

    def bwd_chunk(s):
        return NCHUNK - 1 - s

    def specs(chunk_fn):
        return [
            pl.BlockSpec((1, CHUNK, D_SSD), lambda b, s: (b, chunk_fn(s), 0)),
            pl.BlockSpec((1, CHUNK, 2 * NSTATE), lambda b, s: (b, chunk_fn(s), D_SSD // (2 * NSTATE))),
            pl.BlockSpec((1, CHUNK, 2 * NSTATE), lambda b, s: (b, chunk_fn(s), D_SSD // (2 * NSTATE) + 1)),
            pl.BlockSpec((1, CHUNK, DT_PAD), lambda b, s: (b, chunk_fn(s), 0)),
        ]

    out = jax.ShapeDtypeStruct((nb, NROW, D_SSD), bf16)
    return pl.pallas_call(
        _ssd_kernel,
        out_shape=(out, out),
        grid=(nb, NCHUNK),
        in_specs=specs(fwd_chunk) + specs(bwd_chunk) + [
            pl.BlockSpec((2, HEADS), lambda b, s: (0, 0)),
            pl.BlockSpec((2, HEADS), lambda b, s: (0, 0)),
            pl.BlockSpec((HEADS, D_SSD), lambda b, s: (0, 0)),
            pl.BlockSpec((HEADS, HEADS), lambda b, s: (0, 0)),
            pl.BlockSpec((1, D_SSD), lambda b, s: (0, 0)),
        ],
        out_specs=(
            pl.BlockSpec((1, CHUNK, D_SSD), lambda b, s: (b, fwd_chunk(s), 0)),
            pl.BlockSpec((1, CHUNK, D_SSD), lambda b, s: (b, bwd_chunk(s), 0)),
        ),
        scratch_shapes=[pltpu.VMEM((2, GROUPS, NSTATE, HPG * HEADDIM), f32), pltpu.VMEM((CHUNK, D_SSD), f32)],
        compiler_params=pltpu.CompilerParams(dimension_semantics=("arbitrary", "arbitrary")),
        name="ssd",
    )(act, act, act, dtp, act, act, act, dtp, dt_bias, a_log, expand, eye, dsk)


def _s5_matrices(lam_re, lam_im, log_step, b_re, b_im, c_re, c_im):
    hp = lax.Precision.HIGHEST
    t = S5_T
    lr = lam_re.astype(f32)
    li = lam_im.astype(f32)
    step = jnp.exp(log_step.astype(f32))[..., None]
    mag = jnp.exp(lr * step)
    ar = mag * jnp.cos(li * step)
    ai = mag * jnp.sin(li * step)
    den = lr * lr + li * li
    fr = ((ar - 1.0) * lr + ai * li) / den
    fi = (ai * lr - (ar - 1.0) * li) / den
    br = b_re.astype(f32)
    bi = b_im.astype(f32)
    bbr = fr[..., None] * br - fi[..., None] * bi
    bbi = fr[..., None] * bi + fi[..., None] * br
    taus = jnp.arange(t + 1, dtype=f32)[:, None]
    pmag = jnp.exp(taus * (lr * step)[:, :, None, :])
    pang = taus * (li * step)[:, :, None, :]
    pwr = pmag * jnp.cos(pang)
    pwi = pmag * jnp.sin(pang)
    cr = c_re.astype(f32)[:, :, None]
    ci = c_im.astype(f32)[:, :, None]
    dr = cr * pwr[:, :, :, None, :] - ci * pwi[:, :, :, None, :]
    di = cr * pwi[:, :, :, None, :] + ci * pwr[:, :, :, None, :]
    kern = (jnp.einsum('dgtkp,dgpj->dgtkj', dr, bbr, precision=hp)
            - jnp.einsum('dgtkp,dgpj->dgtkj', di, bbi, precision=hp))
    tt = jnp.arange(t)[:, None]
    ss = jnp.arange(t)[None, :]
    kf = kern[0][:, jnp.clip(tt - ss, 0, t)]
    kb = kern[1][:, jnp.clip(ss - tt, 0, t)]
    m = (jnp.where((tt >= ss)[None, :, :, None, None], kf, 0.0)
         + jnp.where((ss >= tt)[None, :, :, None, None], kb, 0.0))
    m_t = m.transpose(0, 2, 4, 1, 3).reshape(S5_G, t * S5_C, t * S5_C)
    ef = jnp.arange(t - 1, -1, -1)
    eb = jnp.arange(t)

    def state_map(d, exps):
        pr = pwr[d][:, exps]
        pi = pwi[d][:, exps]
        re = pr[:, :, None, :] * bbr[d].transpose(0, 2, 1)[:, None] - pi[:, :, None, :] * bbi[d].transpose(0, 2, 1)[:, None]
        im = pr[:, :, None, :] * bbi[d].transpose(0, 2, 1)[:, None] + pi[:, :, None, :] * bbr[d].transpose(0, 2, 1)[:, None]
        return re.reshape(S5_G, t * S5_C, S5_P), im.reshape(S5_G, t * S5_C, S5_P)

    pfr, pfi = state_map(0, ef)
    pbr, pbi = state_map(1, eb)

    def read_map(d, exps):
        qr = dr[d][:, exps].transpose(0, 3, 1, 2).reshape(S5_G, S5_P, t * S5_C)
        qi = -di[d][:, exps].transpose(0, 3, 1, 2).reshape(S5_G, S5_P, t * S5_C)
        return qr, qi

    qfr, qfi = read_map(0, jnp.arange(1, t + 1))
    qbr, qbi = read_map(1, jnp.arange(t, 0, -1))

    gl = S5_G // S5_LB
    npiece = S5_LB * S5_Q
    mt = m_t.reshape(S5_LB, gl, t * S5_C, t * S5_C)
    pcat = jnp.concatenate([pfr, pfi, pbr, pbi], axis=-1).reshape(npiece, gl // S5_Q, t * S5_C, 4 * S5_P)
    qcat = jnp.concatenate([qfr, qfi, qbr, qbi], axis=1).reshape(npiece, gl // S5_Q, 4 * S5_P, t * S5_C)
    tr = jnp.repeat(pwr[:, :, t, :].reshape(2, npiece, 1, 128), 4, axis=0).transpose(1, 0, 2, 3).reshape(npiece, 8, 128)
    ti = jnp.repeat(pwi[:, :, t, :].reshape(2, npiece, 1, 128), 4, axis=0).transpose(1, 0, 2, 3).reshape(npiece, 8, 128)
    return mt.astype(bf16), pcat.astype(bf16), qcat.astype(bf16), tr, ti


def _s5_place_consts():
    gl = S5_G // S5_LB
    g2 = gl // S5_Q
    cplace = np.zeros((gl, S5_T, S5_C, S5_T, gl, S5_C), np.float32)
    for g in range(gl):
        for t in range(S5_T):
            cplace[g, t, np.arange(S5_C), t, g, np.arange(S5_C)] = 1.0
    cstate = np.zeros((g2, 4, S5_P, 4, g2, S5_P), np.float32)
    for g in range(g2):
        for a in range(4):
            cstate[g, a, np.arange(S5_P), a, g, np.arange(S5_P)] = 1.0
    return (jnp.asarray(cplace.reshape(gl, S5_T * S5_C, S5_T * gl * S5_C)),
            jnp.asarray(cstate.reshape(g2, 4 * S5_P, 4 * g2 * S5_P)))


def _s5_xcat(u_ref):
    return jnp.concatenate([u_ref[0, pl.ds(t, S5_NCH, stride=S5_T), :] for t in range(S5_T)], axis=-1).astype(bf16)


def _s5a_kernel(u_ref, pc_ref, cs_ref, sr_ref, si_ref, p_scr):
    b = pl.program_id(1)
    g2n = S5_G // S5_LB // S5_Q

    @pl.when(b == 0)
    def _():
        p_scr[...] = jnp.zeros(p_scr.shape, bf16)
        for q in range(S5_Q):
            for g2 in range(g2n):
                tmp = _dot(pc_ref[q, g2], cs_ref[g2]).astype(bf16)
                for s in range(S5_T):
                    r0 = s * 128 + (g2n * q + g2) * S5_C
                    p_scr[q, r0:r0 + S5_C, :] = tmp[s * S5_C:(s + 1) * S5_C, :]

    xcat = _s5_xcat(u_ref)
    for q in range(S5_Q):
        r = _dot(xcat, p_scr[q])
        for d in range(2):
            k = 4 * d + b
            sr_ref[q, pl.ds(k, S5_NCH, stride=8), :] = r[:, 256 * d:256 * d + 128]
            si_ref[q, pl.ds(k, S5_NCH, stride=8), :] = r[:, 256 * d + 128:256 * d + 256]


def _s5a(u, pcat, cstate):
    nb = u.shape[0]
    out = jax.ShapeDtypeStruct((S5_LB * S5_Q, S5_ROWS, 128), f32)
    ospec = pl.BlockSpec((S5_Q, S5_ROWS, 128), lambda l, b: (l, 0, 0))
    return pl.pallas_call(
        _s5a_kernel,
        out_shape=(out, out),
        grid=(S5_LB, nb),
        in_specs=[
            pl.BlockSpec((1, NROW, 128), lambda l, b: (b, 0, l)),
            pl.BlockSpec((S5_Q,) + pcat.shape[1:], lambda l, b: (l, 0, 0, 0)),
            pl.BlockSpec(cstate.shape, lambda l, b: (0, 0, 0)),
        ],
        out_specs=(ospec, ospec),
        scratch_shapes=[pltpu.VMEM((S5_Q, S5_T * 128, 512), bf16)],
        compiler_params=pltpu.CompilerParams(dimension_semantics=("arbitrary", "arbitrary"),
                                             vmem_limit_bytes=VMEM_LIMIT),
        name="s5_state",
    )(u, pcat, cstate)


def _s5scan_kernel(sr_ref, si_ref, tr_ref, ti_ref, hr_ref, hi_ref):
    npc = sr_ref.shape[0]
    is_fwd = lax.broadcasted_iota(jnp.int32, (8, 128), 0) < 4
    is_bwd = jnp.logical_not(is_fwd)

    def body(s, carry):
        cf = jnp.where(s < S5_NCH_CTX, S5_NCH_LAT + s, s - S5_NCH_CTX)
        cb = S5_NCH - 1 - s
        rf = pl.multiple_of(cf * 8, 8)
        rb = pl.multiple_of(cb * 8, 8)
        new = []
        for i in range(npc):
            hr = carry[2 * i]
            hi = carry[2 * i + 1]
            xr = jnp.where(is_fwd, sr_ref[i, pl.ds(rf, 8), :], sr_ref[i, pl.ds(rb, 8), :])
            xi = jnp.where(is_fwd, si_ref[i, pl.ds(rf, 8), :], si_ref[i, pl.ds(rb, 8), :])
            pltpu.store(hr_ref.at[i, pl.ds(rf, 8), :], hr, mask=is_fwd)
            pltpu.store(hr_ref.at[i, pl.ds(rb, 8), :], hr, mask=is_bwd)
            pltpu.store(hi_ref.at[i, pl.ds(rf, 8), :], hi, mask=is_fwd)
            pltpu.store(hi_ref.at[i, pl.ds(rb, 8), :], hi, mask=is_bwd)
            tr = tr_ref[i]
            ti = ti_ref[i]
            new.append(tr * hr - ti * hi + xr)
            new.append(tr * hi + ti * hr + xi)
        return tuple(new)

    z = jnp.zeros((8, 128), f32)
    lax.fori_loop(0, S5_NCH, body, (z,) * (2 * npc))


def _s5scan(sr, si, tr, ti):
    npc = 2
    big = pl.BlockSpec((npc, S5_ROWS, 128), lambda i: (i, 0, 0))
    small = pl.BlockSpec((npc, 8, 128), lambda i: (i, 0, 0))
    out = jax.ShapeDtypeStruct(sr.shape, f32)
    return pl.pallas_call(
        _s5scan_kernel,
        out_shape=(out, out),
        grid=(sr.shape[0] // npc,),
        in_specs=[big, big, small, small],
        out_specs=(big, big),
        compiler_params=pltpu.CompilerParams(vmem_limit_bytes=VMEM_LIMIT),
        name="s5_scan",
    )(sr, si, tr, ti)


def _s5c_kernel(u_ref, hr_ref, hi_ref, mt_ref, qc_ref, cp_ref, y_ref, w_scr, q_scr):
    b = pl.program_id(1)
    gl = S5_G // S5_LB
    g2n = gl // S5_Q

    @pl.when(b == 0)
    def _():
        for g in range(gl):
            tmp = _dot(mt_ref[0, g], cp_ref[g]).astype(bf16)
            for s in range(S5_T):
                w_scr[s * 128 + g * S5_C:s * 128 + (g + 1) * S5_C, :] = tmp[s * S5_C:(s + 1) * S5_C, :]
        for qq in range(S5_Q):
            for g2 in range(g2n):
                tmp = _dot(qc_ref[qq, g2], cp_ref[g2n * qq + g2]).astype(bf16)
                for a in range(4):
                    r0 = a * g2n * S5_P + g2 * S5_P
                    q_scr[qq, r0:r0 + S5_P, :] = tmp[a * S5_P:(a + 1) * S5_P, :]

    y = _dot(_s5_xcat(u_ref), w_scr[...])
    for q in range(S5_Q):
        parts = []
        for d in range(2):
            k = 4 * d + b
            parts.append(hr_ref[q, pl.ds(k, S5_NCH, stride=8), :])
            parts.append(hi_ref[q, pl.ds(k, S5_NCH, stride=8), :])
        hin = jnp.concatenate(parts, axis=-1).astype(bf16)
        y = y + _dot(hin, q_scr[q])
    for t in range(S5_T):
        y_ref[0, pl.ds(t, S5_NCH, stride=S5_T), :] = y[:, 128 * t:128 * (t + 1)]


def _s5c(u, hr, hi, mt, qcat, cplace):
    nb = u.shape[0]
    hspec = pl.BlockSpec((S5_Q, S5_ROWS, 128), lambda l, b: (l, 0, 0), pipeline_mode=pl.Buffered(1))
    lanes = S5_T * 128
    return pl.pallas_call(
        _s5c_kernel,
        out_shape=jax.ShapeDtypeStruct((nb, NROW, D_S5), f32),
        grid=(S5_LB, nb),
        in_specs=[
            pl.BlockSpec((1, NROW, 128), lambda l, b: (b, 0, l)),
            hspec, hspec,
            pl.BlockSpec((1,) + mt.shape[1:], lambda l, b: (l, 0, 0, 0)),
            pl.BlockSpec((S5_Q,) + qcat.shape[1:], lambda l, b: (l, 0, 0, 0)),
            pl.BlockSpec(cplace.shape, lambda l, b: (0, 0, 0)),
        ],
        out_specs=pl.BlockSpec((1, NROW, 128), lambda l, b: (b, 0, l)),
        scratch_shapes=[pltpu.VMEM((lanes, lanes), bf16), pltpu.VMEM((S5_Q, 4 * 128, lanes), bf16)],
        compiler_params=pltpu.CompilerParams(dimension_semantics=("arbitrary", "arbitrary"),
                                             vmem_limit_bytes=VMEM_LIMIT),
        name="s5_read",
    )(u, hr, hi, mt, qcat, cplace)


def _s5_mix(u, mats, places):
    mt, pcat, qcat, tr, ti = mats
    cplace, cstate = places
    sr, si = _s5a(u, pcat, cstate)
    hr, hi = _s5scan(sr, si, tr, ti)
    return _s5c(u, hr, hi, mt, qcat, cplace)


FN_N1 = 8
FN_N2 = SEQ // FN_N1
FN_TILE = 16


def _fnet_consts():
    k = np.arange(FN_N2)
    ang = 2.0 * np.pi * np.outer(k, k) / FN_N2
    g = np.concatenate([np.cos(ang), -np.sin(ang)], axis=0) / np.sqrt(SEQ)
    r = np.arange(FN_N1)[:, None, None]
    kc = np.arange(FN_N2)[None, :, None]
    tw = 2.0 * np.pi * r * kc / SEQ * np.ones((1, 1, FN_HD))
    twc = np.cos(tw).reshape(SEQ, FN_HD)
    tws = np.sin(tw).reshape(SEQ, FN_HD)
    j = np.arange(FN_HD)
    angc = 2.0 * np.pi * np.outer(j, j) / FN_HD
    ch = np.concatenate([np.cos(angc), np.sin(angc)], axis=0) / np.sqrt(FN_HD)
    t = np.arange(CTX)
    angt = 2.0 * np.pi * np.outer(t, t) / CTX
    fc = np.concatenate([np.cos(angt), -np.sin(angt)], axis=0) / np.sqrt(CTX)
    as32 = lambda a: jnp.asarray(a.astype(np.float32))
    return as32(g), as32(twc), as32(tws), as32(ch), as32(fc)


def _dft8(a):
    h = 0.7071067811865476
    add = lambda p, q: (p[0] + q[0], p[1] + q[1])
    sub = lambda p, q: (p[0] - q[0], p[1] - q[1])
    mji = lambda p: (p[1], -p[0])
    w1 = lambda p: ((p[0] + p[1]) * h, (p[1] - p[0]) * h)
    w3 = lambda p: ((p[1] - p[0]) * h, -(p[0] + p[1]) * h)
    b0, b1 = add(a[0], a[4]), sub(a[0], a[4])
    b2, b3 = add(a[2], a[6]), sub(a[2], a[6])
    b4, b5 = add(a[1], a[5]), sub(a[1], a[5])
    b6, b7 = add(a[3], a[7]), sub(a[3], a[7])
    c0, c2 = add(b0, b2), sub(b0, b2)
    c1, c3 = add(b1, mji(b3)), sub(b1, mji(b3))
    d0, d2 = add(b4, b6), sub(b4, b6)
    d1, d3 = add(b5, mji(b7)), sub(b5, mji(b7))
    t1, t2, t3 = w1(d1), mji(d2), w3(d3)
    return [add(c0, d0), add(c1, t1), add(c2, t2), add(c3, t3),
            sub(c0, d0), sub(c1, t1), sub(c2, t2), sub(c3, t3)]


def _fnet_kernel(xa_ref, xb_ref, g_ref, twc_ref, tws_ref, ch_ref, fc_ref, o_ref, yr_ref, yi_ref):
    g = g_ref[...]
    chm = ch_ref[...]
    x_refs = (xa_ref, xb_ref)
    for r in range(FN_N1):
        xr = jnp.concatenate([x[0, pl.ds(r, FN_N2, stride=FN_N1), :] for x in x_refs], axis=-1).astype(bf16)
        y = _dot(g, xr)
        cs = twc_ref[FN_N2 * r:FN_N2 * (r + 1), :]
        sn = tws_ref[FN_N2 * r:FN_N2 * (r + 1), :]
        cs = jnp.concatenate([cs, cs], axis=-1)
        sn = jnp.concatenate([sn, sn], axis=-1)
        yr = y[0:FN_N2]
        yi = y[FN_N2:2 * FN_N2]
        yr_ref[r] = yr * cs + yi * sn
        yi_ref[r] = yi * cs - yr * sn

    def butterfly(i, carry):
        rows = pl.ds(pl.multiple_of(i * FN_TILE, FN_TILE), FN_TILE)
        for half in range(2):
            cols = slice(FN_HD * half, FN_HD * (half + 1))
            out = _dft8([(yr_ref[r, rows, cols], yi_ref[r, rows, cols]) for r in range(FN_N1)])
            for kr in range(FN_N1):
                yr_ref[kr, rows, cols] = out[kr][0]
                yi_ref[kr, rows, cols] = out[kr][1]
        return carry

    lax.fori_loop(0, FN_N2 // FN_TILE, butterfly, 0)

    for kr in range(FN_N1):
        for hd in range(2):
            cols = slice(FN_HD * hd, FN_HD * (hd + 1))
            v = jnp.concatenate([yr_ref[kr, :, cols], yi_ref[kr, :, cols]], axis=-1).astype(bf16)
            o_ref[0, FN_N2 * kr:FN_N2 * (kr + 1), cols] = _dot(v, chm).astype(o_ref.dtype)

    for hd in range(2):
        cols = slice(FN_HD * hd, FN_HD * (hd + 1))
        xc = x_refs[hd][0, SEQ:NROW, :].astype(bf16)
        vc = _dot(fc_ref[...], xc)
        v = jnp.concatenate([vc[0:CTX], vc[CTX:2 * CTX]], axis=-1).astype(bf16)
        o_ref[0, SEQ:NROW, cols] = _dot(v, chm).astype(o_ref.dtype)


def _fnet(fu, consts):
    nb = fu.shape[0]
    g, twc, tws, ch, fc = consts
    full = lambda a: pl.BlockSpec(a.shape, lambda b, h: (0, 0))
    return pl.pallas_call(
        _fnet_kernel,
        out_shape=jax.ShapeDtypeStruct((nb, NROW, D_FNET), bf16),
        grid=(nb, FN_HEADS // 2),
        in_specs=[pl.BlockSpec((1, NROW, FN_HD), lambda b, h: (b, 0, 2 * h)),
                  pl.BlockSpec((1, NROW, FN_HD), lambda b, h: (b, 0, 2 * h + 1)),
                  full(g), full(twc), full(tws), full(ch), full(fc)],
        out_specs=pl.BlockSpec((1, NROW, 2 * FN_HD), lambda b, h: (b, 0, h)),
        scratch_shapes=[pltpu.VMEM((FN_N1, FN_N2, 2 * FN_HD), f32), pltpu.VMEM((FN_N1, FN_N2, 2 * FN_HD), f32)],
        compiler_params=pltpu.CompilerParams(vmem_limit_bytes=VMEM_LIMIT),
        name="fnet",
    )(fu, fu, g, twc, tws, ch, fc)


def _out_kernel(x_ref, yf_ref, yb_ref, z_ref, ys_ref, u_ref, gs_ref, sp_ref, fg_ref,
                mod_ref, gn_ref, sd_ref, wg_ref, bg_ref, fw_ref, fb_ref, wo_ref, gp_ref, o_ref):
    b = pl.program_id(0)
    j = pl.program_id(1)
    row = jnp.where(j == NT - 1, pl.num_programs(0), b)
    gate = mod_ref[pl.ds(row, 1), 2 * D_MODEL:3 * D_MODEL]
    y = yf_ref[0].astype(f32) + yb_ref[0].astype(f32)
    v = y * _silu(z_ref[0].astype(f32))
    gw = D_SSD // GROUPS
    acc = None
    for g in range(GROUPS):
        vg = v[:, gw * g:gw * (g + 1)]
        vg = vg * lax.rsqrt(jnp.mean(vg * vg, axis=-1, keepdims=True) + EPS) * gn_ref[:, gw * g:gw * (g + 1)]
        part = _dot(vg.astype(bf16), wo_ref[gw * g:gw * (g + 1), :])
        acc = part if acc is None else acc + part
    ys = ys_ref[0] + sd_ref[...] * u_ref[0]
    gl = 0.5 * ys * (1.0 + jnp.tanh(0.7978845608028654 * (ys + 0.044715 * ys * ys * ys)))
    s5o = gl * _sigmoid(_dot(gl.astype(bf16), wg_ref[...]) + bg_ref[...]) * _silu(gs_ref[0].astype(f32))
    acc = acc + _dot(s5o.astype(bf16), wo_ref[D_SSD:D_SSD + D_S5, :])
    fo = (_dot(sp_ref[0].astype(bf16), fw_ref[...]) + fb_ref[...]) * _silu(fg_ref[0].astype(f32))
    acc = acc + _dot(fo.astype(bf16), wo_ref[D_SSD + D_S5:D_SSD + D_S5 + D_FNET, :])
    yn = acc * lax.rsqrt(jnp.mean(acc * acc, axis=-1, keepdims=True) + EPS) * gp_ref[...]
    o_ref[0] = x_ref[0] + gate * yn


def _out(x, yf, yb, z, ys, u, gs, sp, fg, mod_l, gn, sd, wg, bg, fw, fb, wo, gp, ntiles):
    nb = x.shape[0]
    row = lambda w: pl.BlockSpec((1, TM, w), lambda b, j: (b, j, 0))
    full = lambda a: pl.BlockSpec(a.shape, lambda b, j: (0, 0))
    return pl.pallas_call(
        _out_kernel,
        out_shape=jax.ShapeDtypeStruct((nb, ntiles * TM, D_MODEL), f32),
        grid=(nb, ntiles),
        in_specs=[row(D_MODEL), row(D_SSD), row(D_SSD), row(D_SSD), row(512), row(512), row(512),
                  row(512), row(512), full(mod_l), full(gn), full(sd), full(wg), full(bg),
                  full(fw), full(fb), full(wo), full(gp)],
        out_specs=row(D_MODEL),
        compiler_params=pltpu.CompilerParams(vmem_limit_bytes=VMEM_LIMIT),
        name="out",
    )(x, yf, yb, z, ys, u, gs, sp, fg, mod_l, gn, sd, wg, bg, fw, fb, wo, gp)


def _pack_w_in(w):
    z, xbc, dt, su, sg, fu, fg = jnp.split(w, (1024, 2560, 2592, 3104, 3616, 4128), axis=-1)
    dt = jnp.pad(dt, ((0, 0), (0, DT_PAD - 2 * HEADS)))
    return jnp.concatenate([z, xbc, su, sg, fu, fg, dt], axis=-1).astype(bf16)


def kernel(x, c, ctx, c_ctx, w_mod, b_mod, g_pre, g_post, w_in, conv_w, conv_b, dt_bias, a_log, d_ssd, g_ssd_norm, s5_lambda_re, s5_lambda_im, s5_log_step, s5_b_re, s5_b_im, s5_c_re, s5_c_im, s5_d, s5_w_glu, s5_b_glu, fnet_w, fnet_b, w_out):
    nb = x.shape[0]
    depth = w_in.shape[0]
    xs = jnp.concatenate([x, ctx], axis=1)
    c8 = jnp.concatenate([c, c_ctx[None, :], jnp.zeros((8 - nb - 1, D_MODEL), f32)], axis=0)
    mod = _mod_all(c8, w_mod, b_mod[:, None, :])
    expand = jnp.repeat(jnp.eye(HEADS, dtype=f32), HEADDIM, axis=1).astype(bf16)
    eye = jnp.eye(HEADS, dtype=f32).astype(bf16)
    fconsts = tuple(a.astype(bf16) if i in (0, 3, 4) else a for i, a in enumerate(_fnet_consts()))
    places = tuple(a.astype(bf16) for a in _s5_place_consts())
    for i in range(depth):
        z, xbc, u, gs, fu, fg, dtp = _inproj(xs, mod[i], g_pre[i][None, :], _pack_w_in(w_in[i]))
        act = _conv(xbc, conv_w[i].reshape(9, D_XBC), conv_b[i][None, :])
        yf, yb = _ssd(act, dtp, dt_bias[i], a_log[i], expand, eye, jnp.repeat(d_ssd[i], HEADDIM)[None, :])
        mats = _s5_matrices(s5_lambda_re[i], s5_lambda_im[i], s5_log_step[i], s5_b_re[i], s5_b_im[i],
                            s5_c_re[i], s5_c_im[i])
        ys = _s5_mix(u, mats, places)
        sp = _fnet(fu, fconsts)
        xs = _out(xs, yf, yb, z, ys, u, gs, sp, fg, mod[i], g_ssd_norm[i][None, :], s5_d[i][None, :],
                  s5_w_glu[i].astype(bf16), s5_b_glu[i][None, :], fnet_w[i].astype(bf16), fnet_b[i][None, :],
                  w_out[i].astype(bf16), g_post[i][None, :], NT if i < depth - 1 else NT - 1)
    return xs
```

```python
import functools

import numpy as np
import jax
import jax.numpy as jnp
from jax import lax
from jax.experimental import pallas as pl
from jax.experimental.pallas import tpu as pltpu

f32 = jnp.float32
bf16 = jnp.bfloat16

D_MODEL = 1024
SEQ = 4096
CTX = 256
NROW = SEQ + CTX
DEPTH = 4
GRID_W = 64
D_SSD = 1024
HEADDIM = 64
HEADS = 16
GROUPS = 2
HPG = HEADS // GROUPS
NSTATE = 128
CHUNK = 128
NCHUNK = NROW // CHUNK
NCHUNK_LAT = SEQ // CHUNK
D_XBC = D_SSD + 2 * GROUPS * NSTATE
D_S5 = 512
S5_G = 32
S5_C = 16
S5_P = 64
S5_T = 8
S5_NCH = NROW // S5_T
S5_NCH_LAT = SEQ // S5_T
S5_NCH_CTX = CTX // S5_T
S5_LB = 4
S5_Q = 4
S5_ROWS = S5_NCH * 8
D_FNET = 512
FN_HEADS = 4
FN_HD = 128
DT_PAD = 128
D_PACK = D_SSD + D_XBC + 4 * 512 + DT_PAD
EPS = 1e-6
TM = 256
NT = NROW // TM
VMEM_LIMIT = 56 * 1024 * 1024


def _sigmoid(v):
    return 0.5 + 0.5 * jnp.tanh(0.5 * v)


def _silu(v):
    half = 0.5 * v
    return half + half * jnp.tanh(half)


def _split2(v):
    hi = v.astype(bf16)
    lo = (v - hi.astype(f32)).astype(bf16)
    return hi, lo


def _split3(v):
    hi = v.astype(bf16)
    r = v - hi.astype(f32)
    mid = r.astype(bf16)
    lo = (r - mid.astype(f32)).astype(bf16)
    return hi, mid, lo


def _dot(a, b):
    return jnp.dot(a, b, preferred_element_type=f32)


def _dot_nt(a, b):
    return lax.dot_general(a, b, (((1,), (1,)), ((), ())), preferred_element_type=f32)


def _mod_kernel(c_ref, w_ref, b_ref, o_ref):
    s = _silu(c_ref[...])
    o_ref[0] = _dot(s.astype(bf16), w_ref[0].astype(bf16)) + b_ref[0]


def _mod_all(c8, w_mod, b_mod):
    depth = w_mod.shape[0]
    return pl.pallas_call(
        _mod_kernel,
        out_shape=jax.ShapeDtypeStruct((depth, 8, 3 * D_MODEL), f32),
        grid=(depth, 3),
        in_specs=[
            pl.BlockSpec((8, D_MODEL), lambda l, n: (0, 0)),
            pl.BlockSpec((1, D_MODEL, D_MODEL), lambda l, n: (l, 0, n)),
            pl.BlockSpec((1, 1, D_MODEL), lambda l, n: (l, 0, n)),
        ],
        out_specs=pl.BlockSpec((1, 8, D_MODEL), lambda l, n: (l, 0, n)),
        name="mod",
    )(c8, w_mod, b_mod)


_PROJ_WIDTHS = (D_SSD, D_XBC, 512, 512, 512, 512, DT_PAD)
_PROJ_DTYPES = (bf16, bf16, f32, bf16, f32, bf16, f32)


def _inproj_kernel(x_ref, mod_ref, g_ref, w_ref, *out_refs):
    b = pl.program_id(0)
    j = pl.program_id(1)
    row = jnp.where(j == NT - 1, pl.num_programs(0), b)
    m = mod_ref[pl.ds(row, 1), :]
    shift = m[:, 0:D_MODEL]
    scale = m[:, D_MODEL:2 * D_MODEL]
    x = x_ref[0]
    ms = jnp.mean(x * x, axis=-1, keepdims=True)
    xn = x * lax.rsqrt(ms + EPS) * g_ref[...]
    h = (xn * (1.0 + scale) + shift).astype(bf16)
    off = 0
    for ref, width in zip(out_refs, _PROJ_WIDTHS):
        ref[0] = _dot(h, w_ref[:, off:off + width]).astype(ref.dtype)
        off += width


def _inproj(x, mod_l, g_pre, w_pack):
    nb = x.shape[0]
    outs = tuple(jax.ShapeDtypeStruct((nb, NROW, w), dt) for w, dt in zip(_PROJ_WIDTHS, _PROJ_DTYPES))
    return pl.pallas_call(
        _inproj_kernel,
        out_shape=outs,
        grid=(nb, NT),
        in_specs=[
            pl.BlockSpec((1, TM, D_MODEL), lambda b, j: (b, j, 0)),
            pl.BlockSpec((8, 3 * D_MODEL), lambda b, j: (0, 0)),
            pl.BlockSpec((1, D_MODEL), lambda b, j: (0, 0)),
            pl.BlockSpec((D_MODEL, D_PACK), lambda b, j: (0, 0)),
        ],
        out_specs=tuple(pl.BlockSpec((1, TM, w), lambda b, j: (b, j, 0)) for w in _PROJ_WIDTHS),
        compiler_params=pltpu.CompilerParams(vmem_limit_bytes=VMEM_LIMIT),
        name="inproj",
    )(x, mod_l, g_pre, w_pack)


def _conv_kernel(x_ref, w_ref, b_ref, o_ref, pad_ref, left_ref, right_ref):
    zero = jnp.zeros((GRID_W, 128), f32)
    for ref in (pad_ref, left_ref, right_ref):
        ref[0:GRID_W, :] = zero
        ref[GRID_W + SEQ:2 * GRID_W + SEQ, :] = zero
    w = w_ref[...]
    bias = b_ref[...]
    rid = lax.broadcasted_iota(jnp.int32, (GRID_W, 128), 0)
    first = rid == 0
    last = rid == GRID_W - 1

    def fill(r, carry):
        v = x_ref[0, pl.ds(pl.multiple_of(r * GRID_W, GRID_W), GRID_W), :].astype(f32)
        dst = pl.ds(pl.multiple_of((r + 1) * GRID_W, GRID_W), GRID_W)
        pad_ref[dst, :] = v
        left_ref[dst, :] = jnp.where(first, 0.0, pltpu.roll(v, 1, 0))
        right_ref[dst, :] = jnp.where(last, 0.0, pltpu.roll(v, GRID_W - 1, 0))
        return carry

    lax.fori_loop(0, SEQ // GRID_W, fill, 0)

    def body(r, carry):
        acc = jnp.zeros((GRID_W, 128), f32) + bias
        for i in range(3):
            rows = pl.ds(pl.multiple_of((r + i) * GRID_W, GRID_W), GRID_W)
            acc = (acc + w[3 * i:3 * i + 1] * left_ref[rows, :] + w[3 * i + 1:3 * i + 2] * pad_ref[rows, :]
                   + w[3 * i + 2:3 * i + 3] * right_ref[rows, :])
        o_ref[0, pl.ds(pl.multiple_of(r * GRID_W, GRID_W), GRID_W), :] = _silu(acc).astype(o_ref.dtype)
        return carry

    lax.fori_loop(0, SEQ // GRID_W, body, 0, unroll=2)

    xc = x_ref[0, SEQ:NROW, :].astype(f32)
    cid =lax.broadcasted_iota(jnp.int32, (CTX, 128), 0)
    vm = jnp.where(cid == 0, 0.0, pltpu.roll(xc, 1, 0))
    vp = jnp.where(cid == CTX - 1, 0.0, pltpu.roll(xc, CTX - 1, 0))
    acc = bias + w[3:4] * vm + w[4:5] * xc + w[5:6] * vp
    o_ref[0, SEQ:NROW, :] = _silu(acc).astype(o_ref.dtype)


def _conv(xbc, w9, cb):
    nb = xbc.shape[0]
    return pl.pallas_call(
        _conv_kernel,
        out_shape=jax.ShapeDtypeStruct((nb, NROW, D_XBC), bf16),
        grid=(nb, D_XBC // 128),
        in_specs=[
            pl.BlockSpec((1, NROW, 128), lambda b, c: (b, 0, c)),
            pl.BlockSpec((9, 128), lambda b, c: (0, c)),
            pl.BlockSpec((1, 128), lambda b, c: (0, c)),
        ],
        out_specs=pl.BlockSpec((1, NROW, 128), lambda b, c: (b, 0, c)),
        scratch_shapes=[pltpu.VMEM((SEQ + 2 * GRID_W, 128), f32)] * 3,
        name="conv",
    )(xbc, w9, cb)


def _ssd_kernel(xf_ref, bf_ref, cf_ref, dtf_ref, xb_ref, bb_ref, cb_ref, dtb_ref,
                dtbias_ref, alog_ref, e_ref, eye_ref, dsk_ref, yf_ref, yb_ref, h_ref, ybuf_ref):
    s = pl.program_id(1)

    @pl.when(s == 0)
    def _():
        h_ref[...] = jnp.zeros(h_ref.shape, f32)

    expand = e_ref[...]
    eye = eye_ref[...]
    ii = lax.broadcasted_iota(jnp.int32, (CHUNK, CHUNK), 0)
    jj = lax.broadcasted_iota(jnp.int32, (CHUNK, CHUNK), 1)
    low_head = (lax.broadcasted_iota(jnp.int32, (CHUNK, D_SSD), 1) & (2 * HEADDIM - 1)) < HEADDIM
    dirs =((xf_ref, bf_ref, cf_ref, dtf_ref, yf_ref), (xb_ref, bb_ref, cb_ref, dtb_ref, yb_ref))
    for d, (x_ref, b_ref, c_ref, dt_ref, y_ref) in enumerate(dirs):
        keep = (ii >= jj) if d == 0 else (ii <= jj)
        tri = keep.astype(bf16)
        x = x_ref[0].astype(f32)
        bm = b_ref[0].astype(f32)
        cm = c_ref[0]
        v = dt_ref[0][:, HEADS * d:HEADS * (d + 1)] + dtbias_ref[d:d + 1, :]
        dt = jnp.maximum(v, 0.0) + jnp.log1p(jnp.exp(-jnp.abs(v)))
        a = dt * (-jnp.exp(alog_ref[d:d + 1, :]))
        a3 = _split3(a)
        acum = _dot(tri, a3[0]) + _dot(tri, a3[1]) + _dot(tri, a3[2])
        total = acum[CHUNK - 1:CHUNK] if d == 0 else acum[0:1]
        c3 = _split3(acum)
        acum_t = _dot_nt(eye, c3[0]) + _dot_nt(eye, c3[1]) + _dot_nt(eye, c3[2])
        coarse = jnp.concatenate([dt, jnp.exp(total - acum)], axis=0).astype(bf16)
        ex = _dot(coarse, expand)
        dt_e = ex[0:CHUNK]
        dte_e = ex[CHUNK:2 * CHUNK]
        fine = jnp.concatenate([jnp.exp(acum), jnp.broadcast_to(jnp.exp(total), (8, HEADS))], axis=0)
        f_hi, f_lo = _split2(fine)
        ex = _dot(f_hi, expand) + _dot(f_lo, expand)
        ea_e = ex[0:CHUNK]
        cd_e = ex[CHUNK:CHUNK + 1]
        xdt = x * dt_e
        xdt_lo = jnp.where(low_head, xdt, 0.0).astype(bf16)
        xdt_hi = jnp.where(low_head, 0.0, xdt).astype(bf16)
        xw_b = (xdt * dte_e).astype(bf16)
        for g in range(GROUPS):
            bg = bm[:, NSTATE * g:NSTATE * (g + 1)]
            cg_b = cm[:, NSTATE * g:NSTATE * (g + 1)].astype(bf16)
            bg_t = bg.T.astype(bf16)
            cbm = _dot(cg_b, bg_t)
            hprev = h_ref[d, g]
            gl = slice(HPG * HEADDIM * g, HPG * HEADDIM * (g + 1))
            yoff = _dot(cg_b, hprev.astype(bf16)) * ea_e[:, gl]
            for pp in range(HPG // 2):
                gmats = []
                for h in (HPG * g + 2 * pp, HPG * g + 2 * pp + 1):
                    seg = acum[:, h:h + 1] - acum_t[h:h + 1, :]
                    dec = jnp.where(keep, jnp.exp(jnp.where(keep, seg, 0.0)), 0.0)
                    gmats.append((cbm * dec).astype(bf16))
                pair = slice(128 * (HPG // 2 * g + pp), 128 * (HPG // 2 * g + pp + 1))
                rhs = jnp.concatenate([xdt_lo[:, pair], xdt_hi[:, pair]], axis=0)
                yd = _dot(jnp.concatenate(gmats, axis=1), rhs)
                ybuf_ref[:, pair] = yd + yoff[:, 128 * pp:128 * (pp + 1)]
            st = _dot(bg_t, xw_b[:, gl])
            h_ref[d, g] = hprev * cd_e[:, gl] + st
        y = ybuf_ref[...]
        if d == 0:
            y = y + dsk_ref[...] * x
        y_ref[0] = y.astype(y_ref.dtype)


def _ssd(act, dtp, dt_bias, a_log, expand, eye, dsk):
    nb = act.shape[0]

    def fwd_chunk(s):
        return jnp.where(s < 2, NCHUNK_LAT + s, s - 2)

    def bwd_chunk(s):
        return NCHUNK - 1 - s

    def specs(chunk_fn):
        return [
            pl.BlockSpec((1, CHUNK, D_SSD), lambda b, s: (b, chunk_fn(s), 0)),
            pl.BlockSpec((1, CHUNK, 2 * NSTATE), lambda b, s: (b, chunk_fn(s), D_SSD // (2 * NSTATE))),
            pl.BlockSpec((1, CHUNK, 2 * NSTATE), lambda b, s: (b, chunk_fn(s), D_SSD // (2 * NSTATE) + 1)),
            pl.BlockSpec((1, CHUNK, DT_PAD), lambda b, s: (b, chunk_fn(s), 0)),
        ]

    out = jax.ShapeDtypeStruct((nb, NROW, D_SSD), bf16)
    return pl.pallas_call(
        _ssd_kernel,
        out_shape=(out, out),
        grid=(nb, NCHUNK),
        in_specs=specs(fwd_chunk) + specs(bwd_chunk) + [
            pl.BlockSpec((2, HEADS), lambda b, s: (0, 0)),
            pl.BlockSpec((2, HEADS), lambda b, s: (0, 0)),
            pl.BlockSpec((HEADS, D_SSD), lambda b, s: (0, 0)),
            pl.BlockSpec((HEADS, HEADS), lambda b, s: (0, 0)),
            pl.BlockSpec((1, D_SSD), lambda b, s: (0, 0)),
        ],
        out_specs=(
            pl.BlockSpec((1, CHUNK, D_SSD), lambda b, s: (b, fwd_chunk(s), 0)),
            pl.BlockSpec((1, CHUNK, D_SSD), lambda b, s: (b, bwd_chunk(s), 0)),
        ),
        scratch_shapes=[pltpu.VMEM((2, GROUPS, NSTATE, HPG * HEADDIM), f32), pltpu.VMEM((CHUNK, D_SSD), f32)],
        compiler_params=pltpu.CompilerParams(dimension_semantics=("arbitrary", "arbitrary")),
        name="ssd",
    )(act, act, act, dtp, act, act, act, dtp, dt_bias, a_log, expand, eye, dsk)


def _s5_matrices(lam_re, lam_im, log_step, b_re, b_im, c_re, c_im):
    hp = lax.Precision.HIGHEST
    t = S5_T
    lr = lam_re.astype(f32)
    li = lam_im.astype(f32)
    step = jnp.exp(log_step.astype(f32))[..., None]
    mag = jnp.exp(lr * step)
    ar = mag * jnp.cos(li * step)
    ai = mag * jnp.sin(li * step)
    den = lr * lr + li * li
    fr = ((ar - 1.0) * lr + ai * li) / den
    fi = (ai * lr - (ar - 1.0) * li) / den
    br = b_re.astype(f32)
    bi = b_im.astype(f32)
    bbr = fr[..., None] * br - fi[..., None] * bi
    bbi = fr[..., None] * bi + fi[..., None] * br
    taus = jnp.arange(t + 1, dtype=f32)[:, None]
    pmag = jnp.exp(taus * (lr * step)[:, :, None, :])
    pang = taus * (li * step)[:, :, None, :]
    pwr = pmag * jnp.cos(pang)
    pwi = pmag * jnp.sin(pang)
    cr = c_re.astype(f32)[:, :, None]
    ci = c_im.astype(f32)[:, :, None]
    dr = cr * pwr[:, :, :, None, :] - ci * pwi[:, :, :, None, :]
    di = cr * pwi[:, :, :, None, :] + ci * pwr[:, :, :, None, :]
    kern = (jnp.einsum('dgtkp,dgpj->dgtkj', dr, bbr, precision=hp)
            - jnp.einsum('dgtkp,dgpj->dgtkj', di, bbi, precision=hp))
    def toeplitz_block(tq, sq):
        if tq > sq:
            return kern[0][:, tq - sq]
        if sq > tq:
            return kern[1][:, sq - tq]
        return kern[0][:, 0] + kern[1][:, 0]

    m = jnp.stack([jnp.stack([toeplitz_block(tq, sq) for sq in range(t)], axis=1) for tq in range(t)], axis=1)
    m_t = m.transpose(0, 2, 4, 1, 3).reshape(S5_G, t * S5_C, t * S5_C)
    ef = slice(t - 1, None, -1)
    eb = slice(0, t)

    def state_map(d, exps):
        pr = pwr[d][:, exps]
        pi = pwi[d][:, exps]
        re = pr[:, :, None, :] * bbr[d].transpose(0, 2, 1)[:, None] - pi[:, :, None, :] * bbi[d].transpose(0, 2, 1)[:, None]
        im = pr[:, :, None, :] * bbi[d].transpose(0, 2, 1)[:, None] + pi[:, :, None, :] * bbr[d].transpose(0, 2, 1)[:, None]
        return re.reshape(S5_G, t * S5_C, S5_P), im.reshape(S5_G, t * S5_C, S5_P)

    pfr, pfi = state_map(0, ef)
    pbr, pbi = state_map(1, eb)

    def read_map(d, exps):
        qr = dr[d][:, exps].transpose(0, 3, 1, 2).reshape(S5_G, S5_P, t * S5_C)
        qi = -di[d][:, exps].transpose(0, 3, 1, 2).reshape(S5_G, S5_P, t * S5_C)
        return qr, qi

    qfr, qfi = read_map(0, slice(1, t + 1))
    qbr, qbi = read_map(1, slice(t, 0, -1))

    gl = S5_G // S5_LB
    npiece = S5_LB * S5_Q
    mt = m_t.reshape(S5_LB, gl, t * S5_C, t * S5_C)
    pcat = jnp.concatenate([pfr, pfi, pbr, pbi], axis=-1).reshape(npiece, gl // S5_Q, t * S5_C, 4 * S5_P)
    qcat = jnp.concatenate([qfr, qfi, qbr, qbi], axis=1).reshape(npiece, gl // S5_Q, 4 * S5_P, t * S5_C)
    tr = jnp.repeat(pwr[:, :, t, :].reshape(2, npiece, 1, 128), 4, axis=0).transpose(1, 0, 2, 3).reshape(npiece, 8, 128)
    ti = jnp.repeat(pwi[:, :, t, :].reshape(2, npiece, 1, 128), 4, axis=0).transpose(1, 0, 2, 3).reshape(npiece, 8, 128)
    return mt.astype(bf16), pcat.astype(bf16), qcat.astype(bf16), tr, ti


def _s5_place_consts():
    gl = S5_G // S5_LB
    g2 = gl // S5_Q
    cplace = np.zeros((gl, S5_T, S5_C, S5_T, gl, S5_C), np.float32)
    for g in range(gl):
        for t in range(S5_T):
            cplace[g, t, np.arange(S5_C), t, g, np.arange(S5_C)] = 1.0
    cstate = np.zeros((g2, 4, S5_P, 4, g2, S5_P), np.float32)
    for g in range(g2):
        for a in range(4):
            cstate[g, a, np.arange(S5_P), a, g, np.arange(S5_P)] = 1.0
    return (jnp.asarray(cplace.reshape(gl, S5_T * S5_C, S5_T * gl * S5_C)),
            jnp.asarray(cstate.reshape(g2, 4 * S5_P, 4 * g2 * S5_P)))


def _s5_xcat(u_ref):
    return jnp.concatenate([u_ref[0, pl.ds(t, S5_NCH, stride=S5_T), :] for t in range(S5_T)], axis=-1).astype(bf16)


def _s5a_kernel(u_ref, pc_ref, cs_ref, sr_ref, si_ref, p_scr):
    b = pl.program_id(1)
    g2n = S5_G // S5_LB // S5_Q

    @pl.when(b == 0)
    def _():
        p_scr[...] = jnp.zeros(p_scr.shape, bf16)
        for q in range(S5_Q):
            for g2 in range(g2n):
                tmp = _dot(pc_ref[q, g2], cs_ref[g2]).astype(bf16)
                for s in range(S5_T):
                    r0 = s * 128 + (g2n * q + g2) * S5_C
                    p_scr[q, r0:r0 + S5_C, :] = tmp[s * S5_C:(s + 1) * S5_C, :]

    xcat = _s5_xcat(u_ref)
    for q in range(S5_Q):
        r = _dot(xcat, p_scr[q])
        for d in range(2):
            k = 4 * d + b
            sr_ref[q, pl.ds(k, S5_NCH, stride=8), :] = r[:, 256 * d:256 * d + 128]
            si_ref[q, pl.ds(k, S5_NCH, stride=8), :] = r[:, 256 * d + 128:256 * d + 256]


def _s5a(u, pcat, cstate):
    nb = u.shape[0]
    out = jax.ShapeDtypeStruct((S5_LB * S5_Q, S5_ROWS, 128), f32)
    ospec = pl.BlockSpec((S5_Q, S5_ROWS, 128), lambda l, b: (l, 0, 0))
    return pl.pallas_call(
        _s5a_kernel,
        out_shape=(out, out),
        grid=(S5_LB, nb),
        in_specs=[
            pl.BlockSpec((1, NROW, 128), lambda l, b: (b, 0, l)),
            pl.BlockSpec((S5_Q,) + pcat.shape[1:], lambda l, b: (l, 0, 0, 0)),
            pl.BlockSpec(cstate.shape, lambda l, b: (0, 0, 0)),
        ],
        out_specs=(ospec, ospec),
        scratch_shapes=[pltpu.VMEM((S5_Q, S5_T * 128, 512), bf16)],
        compiler_params=pltpu.CompilerParams(dimension_semantics=("arbitrary", "arbitrary"),
                                             vmem_limit_bytes=VMEM_LIMIT),
        name="s5_state",
    )(u, pcat, cstate)


def _s5scan_kernel(sr_ref, si_ref, tr_ref, ti_ref, hr_ref, hi_ref):
    npc = sr_ref.shape[0]
    is_fwd = lax.broadcasted_iota(jnp.int32, (8, 128), 0) < 4
    is_bwd = jnp.logical_not(is_fwd)

    def body(s, carry):
        cf = jnp.where(s < S5_NCH_CTX, S5_NCH_LAT + s, s - S5_NCH_CTX)
        cb = S5_NCH - 1 - s
        rf = pl.multiple_of(cf * 8, 8)
        rb = pl.multiple_of(cb * 8, 8)
        new = []
        for i in range(npc):
            hr = carry[2 * i]
            hi = carry[2 * i + 1]
            xr = jnp.where(is_fwd, sr_ref[i, pl.ds(rf, 8), :], sr_ref[i, pl.ds(rb, 8), :])
            xi = jnp.where(is_fwd, si_ref[i, pl.ds(rf, 8), :], si_ref[i, pl.ds(rb, 8), :])
            pltpu.store(hr_ref.at[i, pl.ds(rf, 8), :], hr, mask=is_fwd)
            pltpu.store(hr_ref.at[i, pl.ds(rb, 8), :], hr, mask=is_bwd)
            pltpu.store(hi_ref.at[i, pl.ds(rf, 8), :], hi, mask=is_fwd)
            pltpu.store(hi_ref.at[i, pl.ds(rb, 8), :], hi, mask=is_bwd)
            tr = tr_ref[i]
            ti = ti_ref[i]
            new.append(tr * hr - ti * hi + xr)
            new.append(tr * hi + ti * hr + xi)
        return tuple(new)

    z = jnp.zeros((8, 128), f32)
    lax.fori_loop(0, S5_NCH, body, (z,) * (2 * npc))


def _s5scan(sr, si, tr, ti):
    npc = 2
    big = pl.BlockSpec((npc, S5_ROWS, 128), lambda i: (i, 0, 0))
    small = pl.BlockSpec((npc, 8, 128), lambda i: (i, 0, 0))
    out = jax.ShapeDtypeStruct(sr.shape, f32)
    return pl.pallas_call(
        _s5scan_kernel,
        out_shape=(out, out),
        grid=(sr.shape[0] // npc,),
        in_specs=[big, big, small, small],
        out_specs=(big, big),
        compiler_params=pltpu.CompilerParams(vmem_limit_bytes=VMEM_LIMIT),
        name="s5_scan",
    )(sr, si, tr, ti)


def _s5c_kernel(u_ref, hr_ref, hi_ref, mt_ref, qc_ref, cp_ref, y_ref, w_scr, q_scr):
    b = pl.program_id(1)
    gl = S5_G // S5_LB
    g2n = gl // S5_Q

    @pl.when(b == 0)
    def _():
        for g in range(gl):
            tmp = _dot(mt_ref[0, g], cp_ref[g]).astype(bf16)
            for s in range(S5_T):
                w_scr[s * 128 + g * S5_C:s * 128 + (g + 1) * S5_C, :] = tmp[s * S5_C:(s + 1) * S5_C, :]
        for qq in range(S5_Q):
            for g2 in range(g2n):
                tmp = _dot(qc_ref[qq, g2], cp_ref[g2n * qq + g2]).astype(bf16)
                for a in range(4):
                    r0 = a * g2n * S5_P + g2 * S5_P
                    q_scr[qq, r0:r0 + S5_P, :] = tmp[a * S5_P:(a + 1) * S5_P, :]

    y = _dot(_s5_xcat(u_ref), w_scr[...])
    for q in range(S5_Q):
        parts = []
        for d in range(2):
            k = 4 * d + b
            parts.append(hr_ref[q, pl.ds(k, S5_NCH, stride=8), :])
            parts.append(hi_ref[q, pl.ds(k, S5_NCH, stride=8), :])
        hin = jnp.concatenate(parts, axis=-1).astype(bf16)
        y = y + _dot(hin, q_scr[q])
    for t in range(S5_T):
        y_ref[0, pl.ds(t, S5_NCH, stride=S5_T), :] = y[:, 128 * t:128 * (t + 1)]


def _s5c(u, hr, hi, mt, qcat, cplace):
    nb = u.shape[0]
    hspec = pl.BlockSpec((S5_Q, S5_ROWS, 128), lambda l, b: (l, 0, 0), pipeline_mode=pl.Buffered(1))
    lanes = S5_T * 128
    return pl.pallas_call(
        _s5c_kernel,
        out_shape=jax.ShapeDtypeStruct((nb, NROW, D_S5), f32),
        grid=(S5_LB, nb),
        in_specs=[
            pl.BlockSpec((1, NROW, 128), lambda l, b: (b, 0, l)),
            hspec, hspec,
            pl.BlockSpec((1,) + mt.shape[1:], lambda l, b: (l, 0, 0, 0)),
            pl.BlockSpec((S5_Q,) + qcat.shape[1:], lambda l, b: (l, 0, 0, 0)),
            pl.BlockSpec(cplace.shape, lambda l, b: (0, 0, 0)),
        ],
        out_specs=pl.BlockSpec((1, NROW, 128), lambda l, b: (b, 0, l)),
        scratch_shapes=[pltpu.VMEM((lanes, lanes), bf16), pltpu.VMEM((S5_Q, 4 * 128, lanes), bf16)],
        compiler_params=pltpu.CompilerParams(dimension_semantics=("arbitrary", "arbitrary"),
                                             vmem_limit_bytes=VMEM_LIMIT),
        name="s5_read",
    )(u, hr, hi, mt, qcat, cplace)


def _s5_mix(u, mats, places):
    mt, pcat, qcat, tr, ti = mats
    cplace, cstate = places
    sr, si = _s5a(u, pcat, cstate)
    hr, hi = _s5scan(sr, si, tr, ti)
    return _s5c(u, hr, hi, mt, qcat, cplace)


FN_N1 = 8
FN_N2 = SEQ // FN_N1
FN_TILE = 16


def _fnet_consts():
    k = np.arange(FN_N2)
    ang = 2.0 * np.pi * np.outer(k, k) / FN_N2
    g = np.concatenate([np.cos(ang), -np.sin(ang)], axis=0) / np.sqrt(SEQ)
    r = np.arange(FN_N1)[:, None, None]
    kc = np.arange(FN_N2)[None, :, None]
    tw = 2.0 * np.pi * r * kc / SEQ * np.ones((1, 1, FN_HD))
    twc = np.cos(tw).reshape(SEQ, FN_HD)
    tws = np.sin(tw).reshape(SEQ, FN_HD)
    j = np.arange(FN_HD)
    angc = 2.0 * np.pi * np.outer(j, j) / FN_HD
    ch = np.concatenate([np.cos(angc), np.sin(angc)], axis=0) / np.sqrt(FN_HD)
    t = np.arange(CTX)
    angt = 2.0 * np.pi * np.outer(t, t) / CTX
    fc = np.concatenate([np.cos(angt), -np.sin(angt)], axis=0) / np.sqrt(CTX)
    as32 = lambda a: jnp.asarray(a.astype(np.float32))
    return as32(g), as32(twc), as32(tws), as32(ch), as32(fc)


def _dft8(a):
    h = 0.7071067811865476
    add = lambda p, q: (p[0] + q[0], p[1] + q[1])
    sub = lambda p, q: (p[0] - q[0], p[1] - q[1])
    mji = lambda p: (p[1], -p[0])
    w1 = lambda p: ((p[0] + p[1]) * h, (p[1] - p[0]) * h)
    w3 = lambda p: ((p[1] - p[0]) * h, -(p[0] + p[1]) * h)
    b0, b1 = add(a[0], a[4]), sub(a[0], a[4])
    b2, b3 = add(a[2], a[6]), sub(a[2], a[6])
    b4, b5 = add(a[1], a[5]), sub(a[1], a[5])
    b6, b7 = add(a[3], a[7]), sub(a[3], a[7])
    c0, c2 = add(b0, b2), sub(b0, b2)
    c1, c3 = add(b1, mji(b3)), sub(b1, mji(b3))
    d0, d2 = add(b4, b6), sub(b4, b6)
    d1, d3 = add(b5, mji(b7)), sub(b5, mji(b7))
    t1, t2, t3 = w1(d1), mji(d2), w3(d3)
    return [add(c0, d0), add(c1, t1), add(c2, t2), add(c3, t3),
            sub(c0, d0), sub(c1, t1), sub(c2, t2), sub(c3, t3)]


def _fnet_kernel(xa_ref, xb_ref, g_ref, twc_ref, tws_ref, ch_ref, fc_ref, o_ref, yr_ref, yi_ref):
    g = g_ref[...]
    chm = ch_ref[...]
    x_refs = (xa_ref, xb_ref)
    for r in range(FN_N1):
        xr = jnp.concatenate([x[0, pl.ds(r, FN_N2, stride=FN_N1), :] for x in x_refs], axis=-1).astype(bf16)
        y = _dot(g, xr)
        cs = twc_ref[FN_N2 * r:FN_N2 * (r + 1), :]
        sn = tws_ref[FN_N2 * r:FN_N2 * (r + 1), :]
        cs = jnp.concatenate([cs, cs], axis=-1)
        sn = jnp.concatenate([sn, sn], axis=-1)
        yr = y[0:FN_N2]
        yi = y[FN_N2:2 * FN_N2]
        yr_ref[r] = yr * cs + yi * sn
        yi_ref[r] = yi * cs - yr * sn

    def butterfly(i, carry):
        rows = pl.ds(pl.multiple_of(i * FN_TILE, FN_TILE), FN_TILE)
        for half in range(2):
            cols = slice(FN_HD * half, FN_HD * (half + 1))
            out = _dft8([(yr_ref[r, rows, cols], yi_ref[r, rows, cols]) for r in range(FN_N1)])
            for kr in range(FN_N1):
                yr_ref[kr, rows, cols] = out[kr][0]
                yi_ref[kr, rows, cols] = out[kr][1]
        return carry

    lax.fori_loop(0, FN_N2 // FN_TILE, butterfly, 0)

    for kr in range(FN_N1):
        for hd in range(2):
            cols = slice(FN_HD * hd, FN_HD * (hd + 1))
            v = jnp.concatenate([yr_ref[kr, :, cols], yi_ref[kr, :, cols]], axis=-1).astype(bf16)
            o_ref[0, FN_N2 * kr:FN_N2 * (kr + 1), cols] = _dot(v, chm).astype(o_ref.dtype)

    for hd in range(2):
        cols = slice(FN_HD * hd, FN_HD * (hd + 1))
        xc = x_refs[hd][0, SEQ:NROW, :].astype(bf16)
        vc = _dot(fc_ref[...], xc)
        v = jnp.concatenate([vc[0:CTX], vc[CTX:2 * CTX]], axis=-1).astype(bf16)
        o_ref[0, SEQ:NROW, cols] = _dot(v, chm).astype(o_ref.dtype)


def _fnet(fu, consts):
    nb = fu.shape[0]
    g, twc, tws, ch, fc = consts
    full = lambda a: pl.BlockSpec(a.shape, lambda b, h: (0, 0))
    return pl.pallas_call(
        _fnet_kernel,
        out_shape=jax.ShapeDtypeStruct((nb, NROW, D_FNET), bf16),
        grid=(nb, FN_HEADS // 2),
        in_specs=[pl.BlockSpec((1, NROW, FN_HD), lambda b, h: (b, 0, 2 * h)),
                  pl.BlockSpec((1, NROW, FN_HD), lambda b, h: (b, 0, 2 * h + 1)),
                  full(g), full(twc), full(tws), full(ch), full(fc)],
        out_specs=pl.BlockSpec((1, NROW, 2 * FN_HD), lambda b, h: (b, 0, h)),
        scratch_shapes=[pltpu.VMEM((FN_N1, FN_N2, 2 * FN_HD), f32), pltpu.VMEM((FN_N1, FN_N2, 2 * FN_HD), f32)],
        compiler_params=pltpu.CompilerParams(vmem_limit_bytes=VMEM_LIMIT),
        name="fnet",
    )(fu, fu, g, twc, tws, ch, fc)


def _out_kernel(x_ref, yf_ref, yb_ref, z_ref, ys_ref, u_ref, gs_ref, sp_ref, fg_ref,
                mod_ref, gn_ref, sd_ref, wg_ref, bg_ref, fw_ref, fb_ref, wo_ref, gp_ref, o_ref):
    b = pl.program_id(0)
    j = pl.program_id(1)
    row = jnp.where(j == NT - 1, pl.num_programs(0), b)
    gate = mod_ref[pl.ds(row, 1), 2 * D_MODEL:3 * D_MODEL]
    y = yf_ref[0].astype(f32) + yb_ref[0].astype(f32)
    v = y * _silu(z_ref[0].astype(f32))
    gw = D_SSD // GROUPS
    acc = None
    for g in range(GROUPS):
        vg = v[:, gw * g:gw * (g + 1)]
        vg = vg * lax.rsqrt(jnp.mean(vg * vg, axis=-1, keepdims=True) + EPS) * gn_ref[:, gw * g:gw * (g + 1)]
        part = _dot(vg.astype(bf16), wo_ref[gw * g:gw * (g + 1), :])
        acc = part if acc is None else acc + part
    ys = ys_ref[0] + sd_ref[...] * u_ref[0]
    gl = 0.5 * ys * (1.0 + jnp.tanh(0.7978845608028654 * (ys + 0.044715 * ys * ys * ys)))
    s5o = gl * _sigmoid(_dot(gl.astype(bf16), wg_ref[...]) + bg_ref[...]) * _silu(gs_ref[0].astype(f32))
    acc = acc + _dot(s5o.astype(bf16), wo_ref[D_SSD:D_SSD + D_S5, :])
    fo = (_dot(sp_ref[0].astype(bf16), fw_ref[...]) + fb_ref[...]) * _silu(fg_ref[0].astype(f32))
    acc = acc + _dot(fo.astype(bf16), wo_ref[D_SSD + D_S5:D_SSD + D_S5 + D_FNET, :])
    yn = acc * lax.rsqrt(jnp.mean(acc * acc, axis=-1, keepdims=True) + EPS) * gp_ref[...]
    o_ref[0] = x_ref[0] + gate * yn


def _out(x, yf, yb, z, ys, u, gs, sp, fg, mod_l, gn, sd, wg, bg, fw, fb, wo, gp, ntiles):
    nb = x.shape[0]
    row = lambda w: pl.BlockSpec((1, TM, w), lambda b, j: (b, j, 0))
    full = lambda a: pl.BlockSpec(a.shape, lambda b, j: (0, 0))
    return pl.pallas_call(
        _out_kernel,
        out_shape=jax.ShapeDtypeStruct((nb, ntiles * TM, D_MODEL), f32),
        grid=(nb, ntiles),
        in_specs=[row(D_MODEL), row(D_SSD), row(D_SSD), row(D_SSD), row(512), row(512), row(512),
                  row(512), row(512), full(mod_l), full(gn), full(sd), full(wg), full(bg),
                  full(fw), full(fb), full(wo), full(gp)],
        out_specs=row(D_MODEL),
        compiler_params=pltpu.CompilerParams(vmem_limit_bytes=VMEM_LIMIT),
        name="out",
    )(x, yf, yb, z, ys, u, gs, sp, fg, mod_l, gn, sd, wg, bg, fw, fb, wo, gp)


def _pack_w_in(w):
    z, xbc, dt, su, sg, fu, fg = jnp.split(w, (1024, 2560, 2592, 3104, 3616, 4128), axis=-1)
    dt = jnp.pad(dt, ((0, 0), (0, DT_PAD - 2 * HEADS)))
    return jnp.concatenate([z, xbc, su, sg, fu, fg, dt], axis=-1).astype(bf16)


def kernel(x, c, ctx, c_ctx, w_mod, b_mod, g_pre, g_post, w_in, conv_w, conv_b, dt_bias, a_log, d_ssd, g_ssd_norm, s5_lambda_re, s5_lambda_im, s5_log_step, s5_b_re, s5_b_im, s5_c_re, s5_c_im, s5_d, s5_w_glu, s5_b_glu, fnet_w, fnet_b, w_out):
    nb = x.shape[0]
    depth = w_in.shape[0]
    xs = jnp.concatenate([x, ctx], axis=1)
    c8 = jnp.concatenate([c, c_ctx[None, :], jnp.zeros((8 - nb - 1, D_MODEL), f32)], axis=0)
    mod = _mod_all(c8, w_mod, b_mod[:, None, :])
    expand = jnp.repeat(jnp.eye(HEADS, dtype=f32), HEADDIM, axis=1).astype(bf16)
    eye = jnp.eye(HEADS, dtype=f32).astype(bf16)
    fconsts = tuple(a.astype(bf16) if i in (0, 3, 4) else a for i, a in enumerate(_fnet_consts()))
    places = tuple(a.astype(bf16) for a in _s5_place_consts())
    for i in range(depth):
        z, xbc, u, gs, fu, fg, dtp = _inproj(xs, mod[i], g_pre[i][None, :], _pack_w_in(w_in[i]))
        act = _conv(xbc, conv_w[i].reshape(9, D_XBC), conv_b[i][None, :])
        yf, yb = _ssd(act, dtp, dt_bias[i], a_log[i], expand, eye, jnp.repeat(d_ssd[i], HEADDIM)[None, :])
        mats = _s5_matrices(s5_lambda_re[i], s5_lambda_im[i], s5_log_step[i], s5_b_re[i], s5_b_im[i],
                            s5_c_re[i], s5_c_im[i])
        ys = _s5_mix(u, mats, places)
        sp = _fnet(fu, fconsts)
        xs = _out(xs, yf, yb, z, ys, u, gs, sp, fg, mod[i], g_ssd_norm[i][None, :], s5_d[i][None, :],
                  s5_w_glu[i].astype(bf16), s5_b_glu[i][None, :], fnet_w[i].astype(bf16), fnet_b[i][None, :],
                  w_out[i].astype(bf16), g_post[i][None, :], NT if i < depth - 1 else NT - 1)
    return xs
```
